```python
import jax, jax.numpy as jnp
from jax import lax
import numpy as np

D_MODEL = 1024
BATCH = 8
SEQ = 4096
DEPTH = 2
DEC_BATCH = 8
DEC_SEQ = 32
PAST_LEN = 4096

CHUNK = 64
Q_BLOCK = 128
SB_HEADS = 8
SB_HEAD_DIM = 64
SB_WIDTH = SB_HEADS * SB_HEAD_DIM
SSD_HEADS = 8
SSD_HEAD_DIM = 64
SSD_INNER = SSD_HEADS * SSD_HEAD_DIM
SSD_GROUPS = 2
D_STATE = 128
CONV_WIDTH = 4
CONV_DIM = SSD_INNER + 2 * SSD_GROUPS * D_STATE
MIX_WIDTH = SB_WIDTH + SSD_INNER
Q0 = 0
K0 = SB_WIDTH
V0 = 2 * SB_WIDTH
Z0 = 3 * SB_WIDTH
XBC0 = Z0 + SSD_INNER
DT0 = XBC0 + CONV_DIM
IN_PROJ = DT0 + SSD_HEADS
D_FF = 2816
N_EXPERTS = 8
TOP_K = 2
N_DENSE = (DEPTH + 1) // 2
N_MOE = DEPTH // 2
EPS = 1e-6

kernel_name = "hybrid_stickbreak_ssd_stream_step"


def rms_norm(x, w):
    xf = x.astype(jnp.float32)
    y = xf * lax.rsqrt(jnp.mean(xf * xf, axis=-1, keepdims=True) + EPS)
    return (y * w.astype(jnp.float32)).astype(x.dtype)


def stick_breaking(q, k, v, q_pos, k_pos):
    z = jnp.einsum('bqhd,bkhd->bhqk', q, k) * (SB_HEAD_DIM ** -0.5)
    mask = (k_pos[None, :] < q_pos[:, None])[None, None]
    u = jnp.where(mask, jax.nn.softplus(z), 0.0)
    later = lax.cumsum(u, axis=3, reverse=True) - u
    a = jnp.where(mask, jnp.exp(jax.nn.log_sigmoid(z) - later), 0.0)
    return jnp.einsum('bhqk,bkhd->bqhd', a, v)


def sb_prompt(q, k, v):
    b, s, h, d = q.shape
    nb = s // Q_BLOCK
    qb = q.reshape(b, nb, Q_BLOCK, h, d).transpose(1, 0, 2, 3, 4)
    starts = jnp.arange(nb, dtype=jnp.int32) * Q_BLOCK
    k_pos = jnp.arange(s, dtype=jnp.int32)

    def blk(args):
        qi, st = args
        return stick_breaking(qi, k, v, st + jnp.arange(Q_BLOCK, dtype=jnp.int32), k_pos)

    o = lax.map(blk, (qb, starts))
    return o.transpose(1, 0, 2, 3, 4).reshape(b, s, h, d)


def causal_conv(xbc, prev, w, bias):
    full = jnp.concatenate([prev.astype(xbc.dtype), xbc], axis=1)
    out = lax.conv_general_dilated(full, w.astype(full.dtype)[:, None, :], (1,), 'VALID',
                                   dimension_numbers=('NWC', 'WIO', 'NWC'),
                                   feature_group_count=CONV_DIM)
    return jax.nn.silu(out + bias.astype(out.dtype)), full[:, -(CONV_WIDTH - 1):]


def ssd(x, dt, A, Bm, Cm, h0, q):
    b, l, h, p = x.shape
    n = Bm.shape[-1]
    c = l // q
    x = x.reshape(b, c, q, h, p)
    dt = dt.reshape(b, c, q, h)
    Bm = Bm.reshape(b, c, q, h, n)
    Cm = Cm.reshape(b, c, q, h, n)
    a_cum = jnp.cumsum(dt * A, axis=2)
    seg = a_cum[:, :, :, None, :] - a_cum[:, :, None, :, :]
    tri = jnp.tril(jnp.ones((q, q), dtype=bool))[None, None, :, :, None]
    decay = jnp.exp(jnp.where(tri, seg, -jnp.inf))
    xdt = x * dt[..., None]
    scores = jnp.einsum('bcthn,bcshn->bctsh', Cm, Bm) * decay
    y_diag = jnp.einsum('bctsh,bcshp->bcthp', scores, xdt)
    decay_s = jnp.exp(a_cum[:, :, -1:, :] - a_cum)
    states = jnp.einsum('bcshn,bcsh,bcshp->bchpn', Bm, decay_s, xdt)
    chunk_decay = jnp.exp(a_cum[:, :, -1, :])

    def step(hc, inp):
        st, dec = inp
        return dec[..., None, None] * hc + st, hc

    h_final, h_prev = lax.scan(step, h0, (states.transpose(1, 0, 2, 3, 4), chunk_decay.transpose(1, 0, 2)))
    h_prev = h_prev.transpose(1, 0, 2, 3, 4)
    y_off = jnp.einsum('bcthn,bchpn->bcthp', Cm, h_prev) * jnp.exp(a_cum)[..., None]
    return (y_diag + y_off).reshape(b, l, h, p), h_final


def token_mixer(h, l, p, k_past, v_past, ssm_prev, conv_prev):
    b, L, _ = h.shape
    f32 = jnp.float32
    proj = h @ p['w_in'][l]
    q = rms_norm(proj[..., Q0:K0].reshape(b, L, SB_HEADS, SB_HEAD_DIM), p['q_norm_w'][l])
    k = rms_norm(proj[..., K0:V0].reshape(b, L, SB_HEADS, SB_HEAD_DIM), p['k_norm_w'][l])
    v = proj[..., V0:Z0].reshape(b, L, SB_HEADS, SB_HEAD_DIM)
    if k_past is None:
        sb = sb_prompt(q.astype(f32), k.astype(f32), v.astype(f32))
    else:
        past = k_past.shape[1]
        k_all = jnp.concatenate([k_past.astype(f32), k.astype(f32)], axis=1)
        v_all = jnp.concatenate([v_past.astype(f32), v.astype(f32)], axis=1)
        sb = stick_breaking(q.astype(f32), k_all, v_all,
                            past + jnp.arange(L, dtype=jnp.int32),
                            jnp.arange(past + L, dtype=jnp.int32))
    sb = rms_norm(sb.reshape(b, L, SB_WIDTH), p['sb_norm_w'][l]).astype(h.dtype)
    z = proj[..., Z0:XBC0]
    xbc, conv_new = causal_conv(proj[..., XBC0:DT0], conv_prev, p['conv_w'][l], p['conv_b'][l])
    xs = xbc[..., :SSD_INNER].reshape(b, L, SSD_HEADS, SSD_HEAD_DIM).astype(f32)
    gn = SSD_GROUPS * D_STATE
    rep = SSD_HEADS // SSD_GROUPS
    Bm = jnp.repeat(xbc[..., SSD_INNER:SSD_INNER + gn].reshape(b, L, SSD_GROUPS, D_STATE), rep, axis=2)
    Cm = jnp.repeat(xbc[..., SSD_INNER + gn:].reshape(b, L, SSD_GROUPS, D_STATE), rep, axis=2)
    dt = jax.nn.softplus(proj[..., DT0:].astype(f32) + p['dt_bias'][l].astype(f32))
    A = -jnp.exp(p['a_log'][l].astype(f32))
    blk = CHUNK if k_past is None else L
    y, ssm_new = ssd(xs, dt, A, Bm.astype(f32), Cm.astype(f32), ssm_prev.astype(f32), blk)
    y = y + p['d_skip'][l].astype(f32)[:, None] * xs
    y = y.reshape(b, L, SSD_INNER) * jax.nn.silu(z.astype(f32))
    y = rms_norm(y.reshape(b, L, SSD_GROUPS, SSD_INNER // SSD_GROUPS),
                 p['ssd_norm_w'][l].reshape(SSD_GROUPS, -1)).reshape(b, L, SSD_INNER).astype(h.dtype)
    out = jnp.concatenate([sb, y], axis=-1) @ p['w_out'][l]
    return out, k, v, ssm_new.astype(ssm_prev.dtype), conv_new


def swiglu(h, wg, wu, wd):
    return (jax.nn.silu(h @ wg) * (h @ wu)) @ wd


def moe(h, w_router, b_router, wg, wu, wd):
    logits = h.astype(jnp.float32) @ w_router.astype(jnp.float32) + b_router.astype(jnp.float32)
    top_v, top_i = lax.top_k(logits, TOP_K)
    wts = jax.nn.softmax(top_v, axis=-1)
    gates = jnp.sum(jax.nn.one_hot(top_i, N_EXPERTS, dtype=jnp.float32) * wts[..., None], axis=-2)
    out = jnp.zeros(h.shape, jnp.float32)
    for e in range(N_EXPERTS):
        out = out + gates[..., e:e + 1] * swiglu(h, wg[e], wu[e], wd[e]).astype(jnp.float32)
    return out.astype(h.dtype)


def layer(x, c, l, p, k_past, v_past, ssm_prev, conv_prev):
    mod = jax.nn.silu(c) @ p['w_mod'][l] + p['b_mod'][l]
    sh_m, sc_m, g_m, sh_f, sc_f, g_f = jnp.split(mod[:, None, :], 6, axis=-1)
    h = rms_norm(x, p['norm_mix_w'][l]) * (1 + sc_m) + sh_m
    mix, k, v, ssm, conv = token_mixer(h, l, p, k_past, v_past, ssm_prev, conv_prev)
    x = x + g_m * mix
    h = rms_norm(x, p['norm_ffn_w'][l]) * (1 + sc_f) + sh_f
    if l % 2 == 0:
        i = l // 2
        f = swiglu(h, p['w_gate_dense'][i], p['w_up_dense'][i], p['w_down_dense'][i])
    else:
        i = l // 2
        f = moe(h, p['w_router'][i], p['b_router'][i], p['w_gate_moe'][i], p['w_up_moe'][i], p['w_down_moe'][i])
    x = x + g_f * f
    return x, k, v, ssm, conv


def setup_inputs(seed: int = 0) -> dict:
    key = jax.random.key(seed)
    ks = iter(jax.random.split(key, 40))
    f32 = jnp.float32

    def nrm(shape, scale):
        return jax.random.normal(next(ks), shape, f32) * scale

    dt0 = jnp.exp(jax.random.uniform(next(ks), (DEPTH, SSD_HEADS), f32, np.log(1e-3), np.log(1e-1)))
    return {
        'x_prompt': nrm((BATCH, SEQ, D_MODEL), 1.0),
        'x_sample': nrm((DEC_BATCH, DEC_SEQ, D_MODEL), 1.0),
        'c_prompt': nrm((BATCH, D_MODEL), 1.0),
        'c_sample': nrm((DEC_BATCH, D_MODEL), 1.0),
        'cache_sb_k': nrm((DEPTH, DEC_BATCH, PAST_LEN, SB_HEADS, SB_HEAD_DIM), 1.0),
        'cache_sb_v': nrm((DEPTH, DEC_BATCH, PAST_LEN, SB_HEADS, SB_HEAD_DIM), 1.0),
        'state_ssd': nrm((DEPTH, DEC_BATCH, SSD_HEADS, SSD_HEAD_DIM, D_STATE), 0.1),
        'state_conv': nrm((DEPTH, DEC_BATCH, CONV_WIDTH - 1, CONV_DIM), 1.0),
        'w_mod': nrm((DEPTH, D_MODEL, 6 * D_MODEL), D_MODEL ** -0.5),
        'b_mod': nrm((DEPTH, 6 * D_MODEL), 0.01),
        'norm_mix_w': 1.0 + nrm((DEPTH, D_MODEL), 0.02),
        'norm_ffn_w': 1.0 + nrm((DEPTH, D_MODEL), 0.02),
        'w_in': nrm((DEPTH, D_MODEL, IN_PROJ), D_MODEL ** -0.5),
        'q_norm_w': 1.0 + nrm((DEPTH, SB_HEAD_DIM), 0.02),
        'k_norm_w': 1.0 + nrm((DEPTH, SB_HEAD_DIM), 0.02),
        'sb_norm_w': 1.0 + nrm((DEPTH, SB_WIDTH), 0.02),
        'conv_w': nrm((DEPTH, CONV_WIDTH, CONV_DIM), CONV_WIDTH ** -0.5),
        'conv_b': nrm((DEPTH, CONV_DIM), 0.02),
        'dt_bias': dt0 + jnp.log(-jnp.expm1(-dt0)),
        'a_log': jnp.log(jax.random.uniform(next(ks), (DEPTH, SSD_HEADS), f32, 1.0, 16.0)),
        'd_skip': 1.0 + nrm((DEPTH, SSD_HEADS), 0.1),
        'ssd_norm_w': 1.0 + nrm((DEPTH, SSD_INNER), 0.02),
        'w_out': nrm((DEPTH, MIX_WIDTH, D_MODEL), MIX_WIDTH ** -0.5),
        'w_gate_dense': nrm((N_DENSE, D_MODEL, D_FF), D_MODEL ** -0.5),
        'w_up_dense': nrm((N_DENSE, D_MODEL, D_FF), D_MODEL ** -0.5),
        'w_down_dense': nrm((N_DENSE, D_FF, D_MODEL), D_FF ** -0.5),
        'w_router': nrm((N_MOE, D_MODEL, N_EXPERTS), D_MODEL ** -0.5),
        'b_router': nrm((N_MOE, N_EXPERTS), 0.01),
        'w_gate_moe': nrm((N_MOE, N_EXPERTS, D_MODEL, D_FF), D_MODEL ** -0.5),
        'w_up_moe': nrm((N_MOE, N_EXPERTS, D_MODEL, D_FF), D_MODEL ** -0.5),
        'w_down_moe': nrm((N_MOE, N_EXPERTS, D_FF, D_MODEL), D_FF ** -0.5),
    }


def reference(x_prompt, x_sample, c_prompt, c_sample, cache_sb_k, cache_sb_v, state_ssd, state_conv,
              w_mod, b_mod, norm_mix_w, norm_ffn_w, w_in, q_norm_w, k_norm_w, sb_norm_w,
              conv_w, conv_b, dt_bias, a_log, d_skip, ssd_norm_w, w_out,
              w_gate_dense, w_up_dense, w_down_dense,
              w_router, b_router, w_gate_moe, w_up_moe, w_down_moe):
    p = {'w_mod': w_mod, 'b_mod': b_mod, 'norm_mix_w': norm_mix_w, 'norm_ffn_w': norm_ffn_w,
         'w_in': w_in, 'q_norm_w': q_norm_w, 'k_norm_w': k_norm_w, 'sb_norm_w': sb_norm_w,
         'conv_w': conv_w, 'conv_b': conv_b, 'dt_bias': dt_bias, 'a_log': a_log, 'd_skip': d_skip,
         'ssd_norm_w': ssd_norm_w, 'w_out': w_out,
         'w_gate_dense': w_gate_dense, 'w_up_dense': w_up_dense, 'w_down_dense': w_down_dense,
         'w_router': w_router, 'b_router': b_router, 'w_gate_moe': w_gate_moe,
         'w_up_moe': w_up_moe, 'w_down_moe': w_down_moe}
    bp = x_prompt.shape[0]
    xp = x_prompt
    xs = x_sample
    kp_l, vp_l, sp_l, cp_l = [], [], [], []
    ks_l, vs_l, ss_l, cs_l = [], [], [], []
    for l in range(DEPTH):
        ssm0 = jnp.zeros((bp, SSD_HEADS, SSD_HEAD_DIM, D_STATE), state_ssd.dtype)
        conv0 = jnp.zeros((bp, CONV_WIDTH - 1, CONV_DIM), state_conv.dtype)
        xp, k, v, s, c = layer(xp, c_prompt, l, p, None, None, ssm0, conv0)
        kp_l.append(k); vp_l.append(v); sp_l.append(s); cp_l.append(c)
        xs, k, v, s, c = layer(xs, c_sample, l, p, cache_sb_k[l], cache_sb_v[l], state_ssd[l], state_conv[l])
        ks_l.append(k); vs_l.append(v); ss_l.append(s); cs_l.append(c)
    y_prompt = xp
    y_sample = xs
    sb_k_prompt = jnp.stack(kp_l)
    sb_v_prompt = jnp.stack(vp_l)
    ssd_state_prompt = jnp.stack(sp_l)
    conv_state_prompt = jnp.stack(cp_l)
    sb_k_sample = jnp.stack(ks_l)
    sb_v_sample = jnp.stack(vs_l)
    ssd_state_sample = jnp.stack(ss_l)
    conv_state_sample = jnp.stack(cs_l)
    return (y_prompt, y_sample, sb_k_prompt, sb_v_prompt, ssd_state_prompt, conv_state_prompt,
            sb_k_sample, sb_v_sample, ssd_state_sample, conv_state_sample)
```

```python
import functools

import jax
import jax.numpy as jnp
from jax import lax
from jax.experimental import pallas as pl
from jax.experimental.pallas import tpu as pltpu

F32 = jnp.float32
BF16 = jnp.bfloat16
HIGHEST = lax.Precision.HIGHEST

EPS = 1e-6
D_MODEL = 1024
N_HEADS = 8
HEAD_DIM = 64
SB_WIDTH = 512
SSD_INNER = 512
SSD_GROUPS = 2
D_STATE = 128
CONV_WIDTH = 4
CONV_DIM = 1024
IN_MAIN = 3072
DT_PAD = 128
D_FF = 2816
N_EXPERTS = 8
LANES = 128
CONV_HALO = 8

VMEM_LIMIT = 56 * 1024 * 1024


def _cparams(sem, vmem=VMEM_LIMIT):
    return pltpu.CompilerParams(dimension_semantics=sem, vmem_limit_bytes=vmem)


def _silu(x):
    return x * (1.0 / (1.0 + jnp.exp(-x)))


def _softplus(x):
    return jnp.maximum(x, 0.0) + jnp.log(1.0 + jnp.exp(-jnp.abs(x)))


def _rms(x):
    return x * lax.rsqrt(jnp.mean(x * x, axis=-1, keepdims=True) + EPS)


def _mod_kernel(c_ref, w_ref, b_ref, o_ref):
    s = _silu(c_ref[...])
    o_ref[0] = jnp.dot(s, w_ref[0], preferred_element_type=F32, precision=HIGHEST) + b_ref[0]


def _modulation(c_all, w_mod, b_mod):
    depth, _, n6 = w_mod.shape
    ns = c_all.shape[0]
    nj = n6 // D_MODEL
    return pl.pallas_call(
        _mod_kernel,
        grid=(depth, nj),
        in_specs=[pl.BlockSpec((ns, D_MODEL), lambda l, j: (0, 0)),
                  pl.BlockSpec((1, D_MODEL, D_MODEL), lambda l, j: (l, 0, j)),
                  pl.BlockSpec((1, 1, D_MODEL), lambda l, j: (l, 0, j))],
        out_specs=pl.BlockSpec((1, ns, D_MODEL), lambda l, j: (l, 0, j)),
        out_shape=jax.ShapeDtypeStruct((depth, ns, n6), F32),
        compiler_params=_cparams(("arbitrary", "arbitrary")),
        name="adaln_mod",
    )(c_all, w_mod, b_mod.reshape(depth, 1, n6))


def _inproj_kernel(x_ref, sc_ref, sh_ref, nw_ref, w_ref, bd_ref, qw_ref, kw_ref,
                   q_ref, k_ref, v_ref, kf_ref, vf_ref, z_ref, xbc_ref, dt_ref):
    h = _rms(x_ref[...]) * nw_ref[...]
    h = h * (1.0 + sc_ref[0]) + sh_ref[0]
    proj = jnp.dot(h.astype(BF16), w_ref[...], preferred_element_type=F32)

    def head_norm(t, w):
        ss = jnp.dot((t * t).astype(BF16), bd_ref[...], preferred_element_type=F32)
        return t * lax.rsqrt(ss * (1.0 / HEAD_DIM) + EPS) * w

    q = head_norm(proj[:, 0:SB_WIDTH], qw_ref[...])
    k = head_norm(proj[:, SB_WIDTH:2 * SB_WIDTH], kw_ref[...])
    v = proj[:, 2 * SB_WIDTH:3 * SB_WIDTH]
    q_ref[...] = (q * (HEAD_DIM ** -0.5)).astype(BF16)
    k_ref[...] = k.astype(BF16)
    v_ref[...] = v.astype(BF16)
    kf_ref[...] = k
    vf_ref[...] = v
    z_ref[...] = proj[:, 3 * SB_WIDTH:3 * SB_WIDTH + SSD_INNER].astype(BF16)
    xbc_ref[...] = proj[:, 3 * SB_WIDTH + SSD_INNER:IN_MAIN]
    dt_ref[...] = proj[:, IN_MAIN:IN_MAIN + DT_PAD]


def _mod_spec(tm, mr, rows_per_stream):
    if mr == 1:
        return pl.BlockSpec((1, 1, D_MODEL), lambda i: ((i * tm) // rows_per_stream, 0, 0))
    return pl.BlockSpec((1, tm, D_MODEL), lambda i: (i, 0, 0))


def _in_proj(x, sc, sh, nw, w_pad, bd, qw, kw, tm, rows_per_stream):
    t = x.shape[0]
    mr = sc.shape[1]
    row = lambda n: pl.BlockSpec((tm, n), lambda i: (i, 0))
    full = lambda a: pl.BlockSpec(a.shape, lambda i: (0,) * a.ndim)
    outs = [(SB_WIDTH, BF16), (SB_WIDTH, BF16), (SB_WIDTH, BF16), (SB_WIDTH, F32), (SB_WIDTH, F32),
            (SSD_INNER, BF16), (CONV_DIM, F32), (DT_PAD, F32)]
    return pl.pallas_call(
        _inproj_kernel,
        grid=(t // tm,),
        in_specs=[row(D_MODEL), _mod_spec(tm, mr, rows_per_stream), _mod_spec(tm, mr, rows_per_stream),
                  full(nw), full(w_pad), full(bd), full(qw), full(kw)],
        out_specs=[row(n) for n, _ in outs],
        out_shape=[jax.ShapeDtypeStruct((t, n), dt) for n, dt in outs],
        compiler_params=_cparams(("arbitrary",)),
        name="in_proj",
    )(x, sc, sh, nw, w_pad, bd, qw, kw)


def _sb_block(q_heads, kblk, vblk, later_ones, mask, acc_ref, c_ref):
    for hh, qh in enumerate(q_heads):
        z = lax.dot_general(qh, kblk, (((1,), (1,)), ((), ())), preferred_element_type=F32)
        l = jnp.log(1.0 + jnp.exp(-jnp.abs(z)))
        u = jnp.maximum(z, 0.0) + l
        ls = jnp.minimum(z, 0.0) - l
        if mask is not None:
            u = jnp.where(mask, u, 0.0)
        cum = jnp.dot(u.astype(BF16), later_ones, preferred_element_type=F32)
        a = jnp.exp(ls - cum)
        if mask is not None:
            a = jnp.where(mask, a, 0.0)
        pv = jnp.dot(a.astype(BF16), vblk, preferred_element_type=F32)
        c = c_ref[hh]
        acc_ref[hh] += jnp.exp(-c) * pv
        c_ref[hh] = c + cum[:, 0:1] + u[:, 0:1]


def _head_split(q):
    lane = lax.broadcasted_iota(jnp.int32, q.shape, 1)
    zero = jnp.zeros_like(q)
    return (jnp.where(lane < HEAD_DIM, q, zero), jnp.where(lane >= HEAD_DIM, q, zero)), lane


def _strict_lower(n, m):
    r = lax.broadcasted_iota(jnp.int32, (n, m), 0)
    c = lax.broadcasted_iota(jnp.int32, (n, m), 1)
    return c < r


def _attn_prompt_kernel(q_ref, k_ref, v_ref, m_ref, o_ref, acc_ref, c_ref, *, blk):
    i = pl.program_id(2)
    q_heads, lane = _head_split(q_ref[0])
    acc_ref[...] = jnp.zeros_like(acc_ref)
    c_ref[...] = jnp.zeros_like(c_ref)
    later_ones = m_ref[...]

    def kv(kb):
        start = pl.multiple_of(kb * blk, blk)
        return k_ref[0, pl.ds(start, blk), :], v_ref[0, pl.ds(start, blk), :]

    kblk, vblk = kv(i)
    _sb_block(q_heads, kblk, vblk, later_ones, _strict_lower(blk, blk), acc_ref, c_ref)

    def body(j, carry):
        kblk, vblk = kv(i - 1 - j)
        _sb_block(q_heads, kblk, vblk, later_ones, None, acc_ref, c_ref)
        return carry

    lax.fori_loop(0, i, body, 0)
    o_ref[0] = jnp.where(lane < HEAD_DIM, acc_ref[0], acc_ref[1]).astype(o_ref.dtype)


def _attn_prompt(q, k, v, later_ones, blk):
    b, s, _ = q.shape
    npair = SB_WIDTH // LANES
    qspec = pl.BlockSpec((1, blk, LANES), lambda bi, p, i: (bi, i, p))
    kvspec = pl.BlockSpec((1, s, LANES), lambda bi, p, i: (bi, 0, p))
    return pl.pallas_call(
        functools.partial(_attn_prompt_kernel, blk=blk),
        grid=(b, npair, s // blk),
        in_specs=[qspec, kvspec, kvspec, pl.BlockSpec(later_ones.shape, lambda bi, p, i: (0, 0))],
        out_specs=qspec,
        out_shape=jax.ShapeDtypeStruct((b, s, SB_WIDTH), BF16),
        scratch_shapes=[pltpu.VMEM((2, blk, LANES), F32), pltpu.VMEM((2, blk, 1), F32)],
        compiler_params=_cparams(("arbitrary", "arbitrary", "arbitrary")),
        name="sb_attn_prompt",
    )(q, k, v, later_ones)


def _attn_sample_kernel(q_ref, kn_ref, vn_ref, kp_ref, vp_ref, m_ref, o_ref, acc_ref, c_ref, *, blk):
    q_heads, lane = _head_split(q_ref[0])
    tq = q_ref.shape[1]
    acc_ref[...] = jnp.zeros_like(acc_ref)
    c_ref[...] = jnp.zeros_like(c_ref)
    later_ones = m_ref[...]
    _sb_block(q_heads, kn_ref[0], vn_ref[0], later_ones, _strict_lower(tq, blk), acc_ref, c_ref)
    nblk = kp_ref.shape[1] // blk

    def body(j, carry):
        start = pl.multiple_of((nblk - 1 - j) * blk, blk)
        kblk = kp_ref[0, pl.ds(start, blk), :].astype(BF16)
        vblk = vp_ref[0, pl.ds(start, blk), :].astype(BF16)
        _sb_block(q_heads, kblk, vblk, later_ones, None, acc_ref, c_ref)
        return carry

    lax.fori_loop(0, nblk, body, 0)
    o_ref[0] = jnp.where(lane < HEAD_DIM, acc_ref[0], acc_ref[1]).astype(o_ref.dtype)


def _attn_sample(q, k_new, v_new, k_past, v_past, layer, later_ones, blk):
    b, tq, _ = q.shape
    past = k_past.shape[1]
    npair = SB_WIDTH // LANES
    nspec = pl.BlockSpec((1, tq, LANES), lambda bi, p: (bi, 0, p))
    kspec = pl.BlockSpec((1, blk, LANES), lambda bi, p: (bi, 0, p))
    pspec = pl.BlockSpec((1, past, LANES), lambda bi, p: (layer * b + bi, 0, p))
    pad = ((0, 0), (0, blk - tq), (0, 0))
    k_new, v_new = jnp.pad(k_new, pad), jnp.pad(v_new, pad)
    return pl.pallas_call(
        functools.partial(_attn_sample_kernel, blk=blk),
        grid=(b, npair),
        in_specs=[nspec, kspec, kspec, pspec, pspec, pl.BlockSpec(later_ones.shape, lambda bi, p: (0, 0))],
        out_specs=nspec,
        out_shape=jax.ShapeDtypeStruct((b, tq, SB_WIDTH), BF16),
        scratch_shapes=[pltpu.VMEM((2, tq, LANES), F32), pltpu.VMEM((2, tq, 1), F32)],
        compiler_params=_cparams(("arbitrary", "arbitrary")),
        name="sb_attn_sample",
    )(q, k_new, v_new, k_past, v_past, later_ones)


def _ssd_kernel(xbc_ref, z_ref, dt_ref, cw_ref, cb_ref, dtb_ref, alog_ref, dsk_ref, nw_ref, exp_ref,
                s0_ref, c0_ref, y_ref, sT_out_ref, cs_out_ref, ext_ref, st_ref, *, L):
    ci = pl.program_id(1)
    nc = pl.num_programs(1)
    gw = SSD_INNER // SSD_GROUPS
    hpg = N_HEADS // SSD_GROUPS

    @pl.when(ci == 0)
    def _():
        ext_ref[0:CONV_HALO, :] = c0_ref[0]
        st_ref[...] = s0_ref[0].T

    ext_ref[CONV_HALO:CONV_HALO + L, :] = xbc_ref[0]
    conv = cb_ref[...]
    for w in range(CONV_WIDTH):
        off = CONV_HALO - (CONV_WIDTH - 1) + w
        conv = conv + ext_ref[off:off + L, :] * cw_ref[w:w + 1, :]
    xbc = _silu(conv)
    tail = ext_ref[L:L + CONV_HALO, :]
    ext_ref[0:CONV_HALO, :] = tail

    x = xbc[:, 0:SSD_INNER]
    bm = xbc[:, SSD_INNER:SSD_INNER + SSD_GROUPS * D_STATE]
    cm = xbc[:, SSD_INNER + SSD_GROUPS * D_STATE:]
    dt = _softplus(dt_ref[0] + dtb_ref[...])
    da = dt * (-jnp.exp(alog_ref[...]))
    tril = jnp.where(_strict_lower(L, L) | (lax.broadcasted_iota(jnp.int32, (L, L), 0)
                                            == lax.broadcasted_iota(jnp.int32, (L, L), 1)), 1.0, 0.0)
    a_cum = jnp.dot(tril, da, preferred_element_type=F32, precision=HIGHEST)
    a_cum_t = a_cum.T
    a_last = a_cum[L - 1:L, :]
    expand = exp_ref[...]
    bcast = lambda t: jnp.dot(t, expand, preferred_element_type=F32, precision=HIGHEST)
    dt_e = bcast(dt)
    ea_e = bcast(jnp.exp(a_cum))
    ds_e = bcast(jnp.exp(a_last - a_cum))
    xdt = x * dt_e
    wgt = (xdt * ds_e).astype(BF16)
    xdt_b = xdt.astype(BF16)
    bm_t = bm.T.astype(BF16)
    bm_b = bm.astype(BF16)
    cm_b = cm.astype(BF16)
    causal = _strict_lower(L, L) | (lax.broadcasted_iota(jnp.int32, (L, L), 0)
                                    == lax.broadcasted_iota(jnp.int32, (L, L), 1))
    lane_g = lax.broadcasted_iota(jnp.int32, (L, gw), 1)
    y_parts = []
    for g in range(SSD_GROUPS):
        gs = slice(g * gw, (g + 1) * gw)
        ns = slice(g * D_STATE, (g + 1) * D_STATE)
        cb = lax.dot_general(cm_b[:, ns], bm_b[:, ns], (((1,), (1,)), ((), ())), preferred_element_type=F32)
        st_g = st_ref[:, gs]
        y_g = jnp.dot(cm_b[:, ns], st_g.astype(BF16), preferred_element_type=F32) * ea_e[:, gs]
        for hl in range(hpg):
            hd = g * hpg + hl
            seg = a_cum[:, hd:hd + 1] - a_cum_t[hd:hd + 1, :]
            decay = jnp.where(causal, jnp.exp(jnp.minimum(seg, 0.0)), 0.0)
            scores = (cb * decay).astype(BF16)
            xh = jnp.where((lane_g >= hl * HEAD_DIM) & (lane_g < (hl + 1) * HEAD_DIM), xdt_b[:, gs],
                           jnp.zeros_like(xdt_b[:, gs]))
            y_g = y_g + jnp.dot(scores, xh, preferred_element_type=F32)
        y_parts.append(y_g)
        new_states = jnp.dot(bm_t[ns, :], wgt[:, gs], preferred_element_type=F32)
        st_ref[:, gs] = st_g * ea_e[L - 1:L, gs] + new_states
    y = jnp.concatenate(y_parts, axis=-1) + dsk_ref[...] * x
    y = y * _silu(z_ref[0].astype(F32))
    y = jnp.concatenate([_rms(y[:, g * gw:(g + 1) * gw]) for g in range(SSD_GROUPS)], axis=-1) * nw_ref[...]
    y_ref[0] = y.astype(y_ref.dtype)

    @pl.when(ci == nc - 1)
    def _():
        sT_out_ref[0] = st_ref[...].T
        cs_out_ref[0] = tail


def _ssd(xbc, z, dt, cw, cb, dtb, alog, dsk_e, nw, expand, s0, c0, L):
    b, s, _ = xbc.shape
    seq = lambda n: pl.BlockSpec((1, L, n), lambda bi, ci: (bi, ci, 0))
    full = lambda a: pl.BlockSpec(a.shape, lambda bi, ci: (0,) * a.ndim)
    per_b = lambda a: pl.BlockSpec((1,) + a.shape[1:], lambda bi, ci: (bi,) + (0,) * (a.ndim - 1))
    return pl.pallas_call(
        functools.partial(_ssd_kernel, L=L),
        grid=(b, s // L),
        in_specs=[seq(CONV_DIM), seq(SSD_INNER), seq(DT_PAD), full(cw), full(cb), full(dtb), full(alog),
                  full(dsk_e), full(nw), full(expand), per_b(s0), per_b(c0)],
        out_specs=[seq(SSD_INNER), per_b(s0), per_b(c0)],
        out_shape=[jax.ShapeDtypeStruct((b, s, SSD_INNER), BF16),
                   jax.ShapeDtypeStruct(s0.shape, F32), jax.ShapeDtypeStruct(c0.shape, F32)],
        scratch_shapes=[pltpu.VMEM((L + CONV_HALO, CONV_DIM), F32), pltpu.VMEM((D_STATE, SSD_INNER), F32)],
        compiler_params=_cparams(("arbitrary", "arbitrary")),
        name="ssd_mixer",
    )(xbc, z, dt, cw, cb, dtb, alog, dsk_e, nw, expand, s0, c0)


def _outproj_kernel(sb_ref, y_ref, x_ref, sbw_ref, w_ref, g_ref, nw_ref, sc_ref, sh_ref, wr_ref, br_ref,
                    x1_ref, h2_ref, gates_ref, *, with_router):
    sbn = _rms(sb_ref[...].astype(F32)) * sbw_ref[...]
    cat = jnp.concatenate([sbn.astype(BF16), y_ref[...]], axis=-1)
    mix = jnp.dot(cat, w_ref[...], preferred_element_type=F32)
    x1 = x_ref[...] + g_ref[0] * mix
    x1_ref[...] = x1
    h2 = _rms(x1) * nw_ref[...]
    h2 = h2 * (1.0 + sc_ref[0]) + sh_ref[0]
    h2_ref[...] = h2.astype(BF16)
    if not with_router:
        gates_ref[...] = jnp.zeros_like(gates_ref)
        return
    logits = jnp.dot(h2, wr_ref[...], preferred_element_type=F32, precision=HIGHEST) + br_ref[...]
    lane = lax.broadcasted_iota(jnp.int32, logits.shape, 1)
    m1 = jnp.max(logits, axis=-1, keepdims=True)
    i1 = jnp.min(jnp.where(logits == m1, lane, LANES), axis=-1, keepdims=True)
    rest = jnp.where(lane == i1, -jnp.inf, logits)
    m2 = jnp.max(rest, axis=-1, keepdims=True)
    i2 = jnp.min(jnp.where(rest == m2, lane, LANES), axis=-1, keepdims=True)
    w1 = 1.0 / (1.0 + jnp.exp(m2 - m1))
    gates_ref[...] = jnp.where(lane == i1, w1, 0.0) + jnp.where(lane == i2, 1.0 - w1, 0.0)


def _out_proj(sb, y, x, sbw, w_out, g, nw, sc, sh, wr, br, tm, rows_per_stream, with_router):
    t = x.shape[0]
    mr = g.shape[1]
    row = lambda n: pl.BlockSpec((tm, n), lambda i: (i, 0))
    full = lambda a: pl.BlockSpec(a.shape, lambda i: (0,) * a.ndim)
    ms = _mod_spec(tm, mr, rows_per_stream)
    return pl.pallas_call(
        functools.partial(_outproj_kernel, with_router=with_router),
        grid=(t // tm,),
        in_specs=[row(SB_WIDTH), row(SSD_INNER), row(D_MODEL), full(sbw), full(w_out), ms, full(nw), ms, ms,
                  full(wr), full(br)],
        out_specs=[row(D_MODEL), row(D_MODEL), row(LANES)],
        out_shape=[jax.ShapeDtypeStruct((t, D_MODEL), F32), jax.ShapeDtypeStruct((t, D_MODEL), BF16),
                   jax.ShapeDtypeStruct((t, LANES), F32)],
        compiler_params=_cparams(("arbitrary",)),
        name="out_proj",
    )(sb, y, x, sbw, w_out, g, nw, sc, sh, wr, br)


def _ffn_kernel(h_ref, wg_ref, wu_ref, wd_ref, x_ref, g_ref, gates_ref, o_ref, acc_ref, *, n_exp):
    e = pl.program_id(1)
    j = pl.program_id(2)

    @pl.when((e == 0) & (j == 0))
    def _():
        acc_ref[...] = jnp.zeros_like(acc_ref)

    h = h_ref[...]
    gt = jnp.dot(h, wg_ref[0], preferred_element_type=F32)
    up = jnp.dot(h, wu_ref[0], preferred_element_type=F32)
    act = (_silu(gt) * up).astype(BF16)
    part = jnp.dot(act, wd_ref[0], preferred_element_type=F32)
    if n_exp > 1:
        lane = lax.broadcasted_iota(jnp.int32, gates_ref.shape, 1)
        part = part * jnp.sum(jnp.where(lane == e, gates_ref[...], 0.0), axis=-1, keepdims=True)
    acc_ref[...] += part

    @pl.when((e == n_exp - 1) & (j == pl.num_programs(2) - 1))
    def _():
        o_ref[...] = x_ref[...] + g_ref[0] * acc_ref[...]


def _ffn(h, wg, wu, wd, x, g, gates, tm, tf, rows_per_stream):
    t = x.shape[0]
    n_exp = wg.shape[0]
    mr = g.shape[1]
    if mr == 1:
        gspec = pl.BlockSpec((1, 1, D_MODEL), lambda i, e, j: ((i * tm) // rows_per_stream, 0, 0))
    else:
        gspec = pl.BlockSpec((1, tm, D_MODEL), lambda i, e, j: (i, 0, 0))
    row = lambda n: pl.BlockSpec((tm, n), lambda i, e, j: (i, 0))
    return pl.pallas_call(
        functools.partial(_ffn_kernel, n_exp=n_exp),
        grid=(t // tm, n_exp, D_FF // tf),
        in_specs=[row(D_MODEL),
                  pl.BlockSpec((1, D_MODEL, tf), lambda i, e, j: (e, 0, j)),
                  pl.BlockSpec((1, D_MODEL, tf), lambda i, e, j: (e, 0, j)),
                  pl.BlockSpec((1, tf, D_MODEL), lambda i, e, j: (e, j, 0)),
                  row(D_MODEL), gspec, row(LANES)],
        out_specs=row(D_MODEL),
        out_shape=jax.ShapeDtypeStruct((t, D_MODEL), F32),
        scratch_shapes=[pltpu.VMEM((tm, D_MODEL), F32)],
        compiler_params=_cparams(("arbitrary", "arbitrary", "arbitrary")),
        name="ffn_dense" if n_exp == 1 else "ffn_experts",
    )(h, wg, wu, wd, x, g, gates)


def _layer(x, mod, l, p, consts, k_past, v_past, s0, c0, tm, ssd_chunk, attn_blk, expand_mod):
    b, s, _ = x.shape
    t = b * s

    def mod_rows(i):
        m = mod[:, i, :]
        if expand_mod:
            return jnp.repeat(m, s, axis=0).reshape(t // tm, tm, D_MODEL)
        return m.reshape(b, 1, D_MODEL)

    sh_m, sc_m, g_m, sh_f, sc_f, g_f = [mod_rows(i) for i in range(6)]
    xf = x.reshape(t, D_MODEL)
    q, k, v, kf, vf, z, xbc, dt = _in_proj(xf, sc_m, sh_m, p['norm_mix_w'][l], p['w_in'][l], consts['head_ones'],
                                           p['q_norm_w'][l], p['k_norm_w'][l], tm, s)
    shp = lambda a: a.reshape(b, s, a.shape[-1])
    if k_past is None:
        sb = _attn_prompt(shp(q), shp(k), shp(v), consts['later_ones'], attn_blk)
    else:
        sb = _attn_sample(shp(q), shp(k), shp(v), k_past, v_past, l, consts['later_ones'], attn_blk)
    y, s_new, c_new = _ssd(shp(xbc), shp(z), shp(dt), p['conv_w'][l], p['conv_b'][l], p['dt_bias'][l],
                           p['a_log'][l], p['d_skip'][l], p['ssd_norm_w'][l], consts['head_expand'],
                           s0, c0, ssd_chunk)
    x1, h2, gates = _out_proj(sb.reshape(t, SB_WIDTH), y.reshape(t, SSD_INNER), xf, p['sb_norm_w'][l],
                              p['w_out'][l], g_m, p['norm_ffn_w'][l], sc_f, sh_f,
                              p['w_router'][l // 2], p['b_router'][l // 2], tm, s, l % 2 == 1)
    i = l // 2
    if l % 2 == 0:
        x2 = _ffn(h2, p['w_gate_dense'][i:i + 1], p['w_up_dense'][i:i + 1], p['w_down_dense'][i:i + 1],
                  x1, g_f, gates, tm, D_FF // 2, s)
    else:
        x2 = _ffn(h2, p['w_gate_moe'][i], p['w_up_moe'][i], p['w_down_moe'][i], x1, g_f, gates, tm, D_FF // 2, s)
    return x2.reshape(b, s, D_MODEL), kf, vf, s_new, c_new


def kernel(x_prompt, x_sample, c_prompt, c_sample, cache_sb_k, cache_sb_v, state_ssd, state_conv, w_mod, b_mod, norm_mix_w, norm_ffn_w, w_in, q_norm_w, k_norm_w, sb_norm_w, conv_w, conv_b, dt_bias, a_log, d_skip, ssd_norm_w, w_out, w_gate_dense, w_up_dense, w_down_dense, w_router, b_router, w_gate_moe, w_up_moe, w_down_moe):
    depth = w_in.shape[0]
    bp, sp, _ = x_prompt.shape
    bs, ss, _ = x_sample.shape
    past = cache_sb_k.shape[2]
    n_moe = w_router.shape[0]

    row = lambda a: a.reshape(depth, 1, a.shape[-1])
    lane_pad = lambda a: jnp.pad(a, ((0, 0), (0, LANES - a.shape[-1]))).reshape(a.shape[0], 1, LANES)
    p = {
        'norm_mix_w': row(norm_mix_w), 'norm_ffn_w': row(norm_ffn_w),
        'w_in': jnp.pad(w_in, ((0, 0), (0, 0), (0, IN_MAIN + DT_PAD - w_in.shape[-1]))).astype(BF16),
        'q_norm_w': row(jnp.tile(q_norm_w, (1, N_HEADS))), 'k_norm_w': row(jnp.tile(k_norm_w, (1, N_HEADS))),
        'sb_norm_w': row(sb_norm_w), 'conv_w': conv_w, 'conv_b': row(conv_b),
        'dt_bias': lane_pad(dt_bias), 'a_log': lane_pad(a_log),
        'd_skip': row(jnp.repeat(d_skip, HEAD_DIM, axis=-1)), 'ssd_norm_w': row(ssd_norm_w),
        'w_out': w_out.astype(BF16),
        'w_gate_dense': w_gate_dense.astype(BF16), 'w_up_dense': w_up_dense.astype(BF16),
        'w_down_dense': w_down_dense.astype(BF16),
        'w_router': jnp.pad(w_router, ((0, 0), (0, 0), (0, LANES - N_EXPERTS))),
        'b_router': jnp.pad(b_router, ((0, 0), (0, LANES - N_EXPERTS)), constant_values=-1e30).reshape(n_moe, 1, LANES),
        'w_gate_moe': w_gate_moe.astype(BF16), 'w_up_moe': w_up_moe.astype(BF16),
        'w_down_moe': w_down_moe.astype(BF16),
    }
    attn_blk = 256
    idx = jnp.arange(attn_blk, dtype=jnp.int32)
    wide = jnp.arange(SB_WIDTH, dtype=jnp.int32)
    consts = {
        'later_ones': (idx[:, None] > idx[None, :]).astype(BF16),
        'head_ones': (wide[:, None] // HEAD_DIM == wide[None, :] // HEAD_DIM).astype(BF16),
        'head_expand': (jnp.arange(LANES, dtype=jnp.int32)[:, None] == wide[None, :] // HEAD_DIM).astype(F32),
    }

    c_all = jnp.concatenate([c_prompt, c_sample], axis=0)
    mod = _modulation(c_all, w_mod, b_mod).reshape(depth, bp + bs, 6, D_MODEL)

    kp = cache_sb_k.reshape(depth * bs, past, SB_WIDTH)
    vp = cache_sb_v.reshape(depth * bs, past, SB_WIDTH)
    halo_pad = ((0, 0), (0, 0), (CONV_HALO - (CONV_WIDTH - 1), 0), (0, 0))
    conv_in = jnp.pad(state_conv, halo_pad)
    ssd_in = state_ssd.reshape(depth, bs, N_HEADS * HEAD_DIM, D_STATE)
    zero_s = jnp.zeros((bp, N_HEADS * HEAD_DIM, D_STATE), F32)
    zero_c = jnp.zeros((bp, CONV_HALO, CONV_DIM), F32)

    xp, xs = x_prompt, x_sample
    outs_p, outs_s = [], []
    for l in range(depth):
        xp, k, v, s, c = _layer(xp, mod[l, :bp], l, p, consts, None, None, zero_s, zero_c,
                                tm=512, ssd_chunk=256, attn_blk=attn_blk, expand_mod=False)
        outs_p.append((k.reshape(bp, sp, N_HEADS, HEAD_DIM), v.reshape(bp, sp, N_HEADS, HEAD_DIM),
                       s.reshape(bp, N_HEADS, HEAD_DIM, D_STATE), c[:, CONV_HALO - (CONV_WIDTH - 1):]))
        xs, k, v, s, c = _layer(xs, mod[l, bp:], l, p, consts, kp, vp, ssd_in[l], conv_in[l],
                                tm=bs * ss, ssd_chunk=ss, attn_blk=attn_blk, expand_mod=True)
        outs_s.append((k.reshape(bs, ss, N_HEADS, HEAD_DIM), v.reshape(bs, ss, N_HEADS, HEAD_DIM),
                       s.reshape(bs, N_HEADS, HEAD_DIM, D_STATE), c[:, CONV_HALO - (CONV_WIDTH - 1):]))
    stack = lambda outs, i: jnp.stack([o[i] for o in outs])
    return (xp, xs,
            stack(outs_p, 0), stack(outs_p, 1), stack(outs_p, 2), stack(outs_p, 3),
            stack(outs_s, 0), stack(outs_s, 1), stack(outs_s, 2), stack(outs_s, 3))
```

```python
import functools

import jax
import jax.numpy as jnp
from jax import lax
from jax.experimental import pallas as pl
from jax.experimental.pallas import tpu as pltpu

F32 = jnp.float32
BF16 = jnp.bfloat16
HIGHEST = lax.Precision.HIGHEST

EPS = 1e-6
LOG2_E = 1.4426950408889634
D_MODEL = 1024
N_HEADS = 8
HEAD_DIM = 64
SB_WIDTH = 512
SSD_INNER = 512
SSD_GROUPS = 2
D_STATE = 128
CONV_WIDTH = 4
CONV_DIM = 1024
IN_MAIN = 3072
DT_PAD = 128
D_FF = 2816
N_EXPERTS = 8
LANES = 128
CONV_HALO = 8

VMEM_LIMIT = 56 * 1024 * 1024


def _cparams(sem, vmem=VMEM_LIMIT):
    return pltpu.CompilerParams(dimension_semantics=sem, vmem_limit_bytes=vmem)


def _silu(x):
    return x * (1.0 / (1.0 + jnp.exp2(x * -LOG2_E)))


def _softplus(x):
    return jnp.maximum(x, 0.0) + jnp.log(1.0 + jnp.exp(-jnp.abs(x)))


def _rms(x):
    return x * lax.rsqrt(jnp.mean(x * x, axis=-1, keepdims=True) + EPS)


def _mod_kernel(c_ref, w_ref, b_ref, o_ref):
    s = _silu(c_ref[...])
    o_ref[0] = jnp.dot(s, w_ref[0], preferred_element_type=F32, precision=HIGHEST) + b_ref[0]


def _modulation(c_all, w_mod, b_mod):
    depth, _, n6 = w_mod.shape
    ns = c_all.shape[0]
    nj = n6 // D_MODEL
    return pl.pallas_call(
        _mod_kernel,
        grid=(depth, nj),
        in_specs=[pl.BlockSpec((ns, D_MODEL), lambda l, j: (0, 0)),
                  pl.BlockSpec((1, D_MODEL, D_MODEL), lambda l, j: (l, 0, j)),
                  pl.BlockSpec((1, 1, D_MODEL), lambda l, j: (l, 0, j))],
        out_specs=pl.BlockSpec((1, ns, D_MODEL), lambda l, j: (l, 0, j)),
        out_shape=jax.ShapeDtypeStruct((depth, ns, n6), F32),
        compiler_params=_cparams(("arbitrary", "arbitrary")),
        name="adaln_mod",
    )(c_all, w_mod, b_mod.reshape(depth, 1, n6))


def _inproj_kernel(x_ref, sc_ref, sh_ref, nw_ref, w_ref, bd_ref, qw_ref, kw_ref,
                   q_ref, k_ref, v_ref, kf_ref, vf_ref, z_ref, xbc_ref, dt_ref):
    h = _rms(x_ref[...]) * nw_ref[...]
    h = h * (1.0 + sc_ref[0]) + sh_ref[0]
    proj = jnp.dot(h.astype(BF16), w_ref[...], preferred_element_type=F32)

    def head_norm(t, w):
        ss = jnp.dot((t * t).astype(BF16), bd_ref[...], preferred_element_type=F32)
        return t * lax.rsqrt(ss * (1.0 / HEAD_DIM) + EPS) * w

    q = head_norm(proj[:, 0:SB_WIDTH], qw_ref[...])
    k = head_norm(proj[:, SB_WIDTH:2 * SB_WIDTH], kw_ref[...])
    v = proj[:, 2 * SB_WIDTH:3 * SB_WIDTH]
    q_ref[...] = (q * (HEAD_DIM ** -0.5 * LOG2_E)).astype(BF16)
    k_ref[...] = k.astype(BF16)
    v_ref[...] = v.astype(BF16)
    kf_ref[...] = k
    vf_ref[...] = v
    z_ref[...] = proj[:, 3 * SB_WIDTH:3 * SB_WIDTH + SSD_INNER].astype(BF16)
    xbc_ref[...] = proj[:, 3 * SB_WIDTH + SSD_INNER:IN_MAIN]
    dt_ref[...] = proj[:, IN_MAIN:IN_MAIN + DT_PAD]


def _mod_spec(tm, mr, rows_per_stream):
    if mr == 1:
        return pl.BlockSpec((1, 1, D_MODEL), lambda i: ((i * tm) // rows_per_stream, 0, 0))
    return pl.BlockSpec((1, tm, D_MODEL), lambda i: (i, 0, 0))


def _in_proj(x, sc, sh, nw, w_pad, bd, qw, kw, tm, rows_per_stream):
    t = x.shape[0]
    mr = sc.shape[1]
    row = lambda n: pl.BlockSpec((tm, n), lambda i: (i, 0))
    full = lambda a: pl.BlockSpec(a.shape, lambda i: (0,) * a.ndim)
    outs = [(SB_WIDTH, BF16), (SB_WIDTH, BF16), (SB_WIDTH, BF16), (SB_WIDTH, F32), (SB_WIDTH, F32),
            (SSD_INNER, BF16), (CONV_DIM, F32), (DT_PAD, F32)]
    return pl.pallas_call(
        _inproj_kernel,
        grid=(t // tm,),
        in_specs=[row(D_MODEL), _mod_spec(tm, mr, rows_per_stream), _mod_spec(tm, mr, rows_per_stream),
                  full(nw), full(w_pad), full(bd), full(qw), full(kw)],
        out_specs=[row(n) for n, _ in outs],
        out_shape=[jax.ShapeDtypeStruct((t, n), dt) for n, dt in outs],
        compiler_params=_cparams(("arbitrary",)),
        name="in_proj",
    )(x, sc, sh, nw, w_pad, bd, qw, kw)


def _sb_blocks(qs, blocks, from_ones, acc_ref, c_ref):
    acc = acc_ref[...]
    c = c_ref[...]
    for kblk, vblk, mask in blocks:
        z = lax.dot_general(qs, kblk, (((1,), (1,)), ((), ())), preferred_element_type=F32)
        neg_abs = pltpu.bitcast(pltpu.bitcast(z, jnp.uint32) | jnp.uint32(0x80000000), F32)
        u = jnp.maximum(z, 0.0) + jnp.log(1.0 + jnp.exp2(neg_abs)) * LOG2_E
        if mask is not None:
            u = jnp.where(mask, u, 0.0)
        cum = jnp.dot(u.astype(BF16), from_ones, preferred_element_type=F32)
        a = jnp.exp2(z - cum)
        if mask is not None:
            a = jnp.where(mask, a, 0.0)
        pv = jnp.dot(a.astype(BF16), vblk, preferred_element_type=F32)
        acc = acc + jnp.exp2(-c) * pv
        c = c + cum[:, 0:1]
    acc_ref[...] = acc
    c_ref[...] = c


def _stack_heads(q):
    lane = lax.broadcasted_iota(jnp.int32, q.shape, 1)
    zero = jnp.zeros_like(q)
    return jnp.concatenate([jnp.where(lane < HEAD_DIM, q, zero), jnp.where(lane >= HEAD_DIM, q, zero)], axis=0)


def _unstack_heads(acc, tq):
    lane = lax.broadcasted_iota(jnp.int32, (tq, LANES), 1)
    return jnp.where(lane < HEAD_DIM, acc[0:tq], acc[tq:2 * tq])


def _strict_lower(n, m):
    r = lax.broadcasted_iota(jnp.int32, (n, m), 0)
    c = lax.broadcasted_iota(jnp.int32, (n, m), 1)
    return c < r


def _stacked_causal(tq, tk):
    r = lax.broadcasted_iota(jnp.int32, (2 * tq, tk), 0)
    c = lax.broadcasted_iota(jnp.int32, (2 * tq, tk), 1)
    return c < jnp.where(r >= tq, r - tq, r)


def _attn_prompt_kernel(q_ref, k_ref, v_ref, m_ref, o_ref, acc_ref, c_ref, *, blk):
    i = pl.program_id(2)
    qs = _stack_heads(q_ref[0])
    acc_ref[...] = jnp.zeros_like(acc_ref)
    c_ref[...] = jnp.zeros_like(c_ref)
    from_ones = m_ref[...]
    diag_mask = _stacked_causal(blk, blk)

    def kv(kb, mask=None):
        start = pl.multiple_of(kb * blk, blk)
        return k_ref[0, pl.ds(start, blk), :], v_ref[0, pl.ds(start, blk), :], mask

    def run(top, n):
        _sb_blocks(qs, [kv(top - d) for d in range(n)], from_ones, acc_ref, c_ref)

    @pl.when(i == 0)
    def _():
        _sb_blocks(qs, [kv(i, diag_mask)], from_ones, acc_ref, c_ref)

    @pl.when(i > 0)
    def _():
        _sb_blocks(qs, [kv(i, diag_mask), kv(i - 1)], from_ones, acc_ref, c_ref)

    rest = jnp.maximum(i - 1, 0)
    n4 = rest // 4

    def body(j, carry):
        run(rest - 1 - 4 * j, 4)
        return carry

    lax.fori_loop(0, n4, body, 0)
    tail = rest - 4 * n4

    @pl.when(tail >= 2)
    def _():
        run(tail - 1, 2)

    @pl.when(tail % 2 == 1)
    def _():
        run(0, 1)

    o_ref[0] = _unstack_heads(acc_ref[...], blk).astype(o_ref.dtype)


def _attn_prompt(q, k, v, from_ones, blk):
    b, s, _ = q.shape
    npair = SB_WIDTH // LANES
    qspec = pl.BlockSpec((1, blk, LANES), lambda bi, p, i: (bi, i, p))
    kvspec = pl.BlockSpec((1, s, LANES), lambda bi, p, i: (bi, 0, p))
    return pl.pallas_call(
        functools.partial(_attn_prompt_kernel, blk=blk),
        grid=(b, npair, s // blk),
        in_specs=[qspec, kvspec, kvspec, pl.BlockSpec(from_ones.shape, lambda bi, p, i: (0, 0))],
        out_specs=qspec,
        out_shape=jax.ShapeDtypeStruct((b, s, SB_WIDTH), BF16),
        scratch_shapes=[pltpu.VMEM((2 * blk, LANES), F32), pltpu.VMEM((2 * blk, 1), F32)],
        compiler_params=_cparams(("arbitrary", "arbitrary", "arbitrary")),
        name="sb_attn_prompt",
    )(q, k, v, from_ones)


def _attn_sample_kernel(q_ref, kn_ref, vn_ref, kp_ref, vp_ref, m_ref, o_ref, acc_ref, c_ref, *, blk):
    qs = _stack_heads(q_ref[0])
    tq = q_ref.shape[1]
    acc_ref[...] = jnp.zeros_like(acc_ref)
    c_ref[...] = jnp.zeros_like(c_ref)
    from_ones = m_ref[...]
    nblk = kp_ref.shape[1] // blk

    def past(kb):
        return (kp_ref[0, kb * blk:(kb + 1) * blk, :].astype(BF16),
                vp_ref[0, kb * blk:(kb + 1) * blk, :].astype(BF16), None)

    blocks = [(kn_ref[0], vn_ref[0], _stacked_causal(tq, blk))] + [past(kb) for kb in reversed(range(nblk))]
    _sb_blocks(qs, blocks, from_ones, acc_ref, c_ref)
    o_ref[0] = _unstack_heads(acc_ref[...], tq).astype(o_ref.dtype)


def _attn_sample(q, k_new, v_new, k_past, v_past, layer, from_ones, blk):
    b, tq, _ = q.shape
    past = k_past.shape[1]
    npair = SB_WIDTH // LANES
    nspec = pl.BlockSpec((1, tq, LANES), lambda bi, p: (bi, 0, p))
    kspec = pl.BlockSpec((1, blk, LANES), lambda bi, p: (bi, 0, p))
    pspec = pl.BlockSpec((1, past, LANES), lambda bi, p: (layer * b + bi, 0, p))
    pad = ((0, 0), (0, blk - tq), (0, 0))
    k_new, v_new = jnp.pad(k_new, pad), jnp.pad(v_new, pad)
    return pl.pallas_call(
        functools.partial(_attn_sample_kernel, blk=blk),
        grid=(b, npair),
        in_specs=[nspec, kspec, kspec, pspec, pspec, pl.BlockSpec(from_ones.shape, lambda bi, p: (0, 0))],
        out_specs=nspec,
        out_shape=jax.ShapeDtypeStruct((b, tq, SB_WIDTH), BF16),
        scratch_shapes=[pltpu.VMEM((2 * tq, LANES), F32), pltpu.VMEM((2 * tq, 1), F32)],
        compiler_params=_cparams(("arbitrary", "arbitrary")),
        name="sb_attn_sample",
    )(q, k_new, v_new, k_past, v_past, from_ones)


def _ssd_kernel(xbc_ref, z_ref, dt_ref, cw_ref, cb_ref, dtb_ref, alog_ref, dsk_ref, nw_ref, exp_ref,
                s0_ref, c0_ref, y_ref, sT_out_ref, cs_out_ref, ext_ref, st_ref, *, L):
    ci = pl.program_id(1)
    nc = pl.num_programs(1)
    gw = SSD_INNER // SSD_GROUPS
    hpg = N_HEADS // SSD_GROUPS

    @pl.when(ci == 0)
    def _():
        ext_ref[0:CONV_HALO, :] = c0_ref[0]
        st_ref[...] = s0_ref[0].T

    ext_ref[CONV_HALO:CONV_HALO + L, :] = xbc_ref[0]
    conv = cb_ref[...]
    for w in range(CONV_WIDTH):
        off = CONV_HALO - (CONV_WIDTH - 1) + w
        conv = conv + ext_ref[off:off + L, :] * cw_ref[w:w + 1, :]
    xbc = _silu(conv)
    tail = ext_ref[L:L + CONV_HALO, :]
    ext_ref[0:CONV_HALO, :] = tail

    x = xbc[:, 0:SSD_INNER]
    bm = xbc[:, SSD_INNER:SSD_INNER + SSD_GROUPS * D_STATE]
    cm = xbc[:, SSD_INNER + SSD_GROUPS * D_STATE:]
    dt = _softplus(dt_ref[0] + dtb_ref[...])
    da = dt * (-jnp.exp(alog_ref[...]))
    tril = jnp.where(_strict_lower(L, L) | (lax.broadcasted_iota(jnp.int32, (L, L), 0)
                                            == lax.broadcasted_iota(jnp.int32, (L, L), 1)), 1.0, 0.0)
    a_cum = jnp.dot(tril, da * LOG2_E, preferred_element_type=F32, precision=HIGHEST)
    a_cum_t = a_cum.T
    a_last = a_cum[L - 1:L, :]
    expand = exp_ref[...]

    def bcast(t):
        hi = t.astype(BF16)
        lo = (t - hi.astype(F32)).astype(BF16)
        return (jnp.dot(hi, expand, preferred_element_type=F32) + jnp.dot(lo, expand, preferred_element_type=F32))

    per_head = bcast(jnp.concatenate([dt, jnp.exp2(a_cum), jnp.exp2(a_last - a_cum)], axis=0))
    dt_e, ea_e, ds_e = per_head[0:L], per_head[L:2 * L], per_head[2 * L:3 * L]
    xdt = x * dt_e
    wgt = (xdt * ds_e).astype(BF16)
    xdt_b = xdt.astype(BF16)
    bm_t = bm.T.astype(BF16)
    bm_b = bm.astype(BF16)
    cm_b = cm.astype(BF16)
    causal = _strict_lower(L, L) | (lax.broadcasted_iota(jnp.int32, (L, L), 0)
                                    == lax.broadcasted_iota(jnp.int32, (L, L), 1))
    lane_g = lax.broadcasted_iota(jnp.int32, (L, gw), 1)
    y_parts = []
    for g in range(SSD_GROUPS):
        gs = slice(g * gw, (g + 1) * gw)
        ns = slice(g * D_STATE, (g + 1) * D_STATE)
        cb = lax.dot_general(cm_b[:, ns], bm_b[:, ns], (((1,), (1,)), ((), ())), preferred_element_type=F32)
        st_g = st_ref[:, gs]
        y_g = jnp.dot(cm_b[:, ns], st_g.astype(BF16), preferred_element_type=F32) * ea_e[:, gs]
        for hl in range(hpg):
            hd = g * hpg + hl
            seg = a_cum[:, hd:hd + 1] - a_cum_t[hd:hd + 1, :]
            decay = jnp.where(causal, jnp.exp2(seg), 0.0)
            scores = (cb * decay).astype(BF16)
            xh = jnp.where((lane_g >= hl * HEAD_DIM) & (lane_g < (hl + 1) * HEAD_DIM), xdt_b[:, gs],
                           jnp.zeros_like(xdt_b[:, gs]))
            y_g = y_g + jnp.dot(scores, xh, preferred_element_type=F32)
        y_parts.append(y_g)
        new_states = jnp.dot(bm_t[ns, :], wgt[:, gs], preferred_element_type=F32)
        st_ref[:, gs] = st_g * ea_e[L - 1:L, gs] + new_states
    y = jnp.concatenate(y_parts, axis=-1) + dsk_ref[...] * x
    y = y * _silu(z_ref[0].astype(F32))
    y = jnp.concatenate([_rms(y[:, g * gw:(g + 1) * gw]) for g in range(SSD_GROUPS)], axis=-1) * nw_ref[...]
    y_ref[0] = y.astype(y_ref.dtype)

    @pl.when(ci == nc - 1)
    def _():
        sT_out_ref[0] = st_ref[...].T
        cs_out_ref[0] = tail


def _ssd(xbc, z, dt, cw, cb, dtb, alog, dsk_e, nw, expand, s0, c0, L):
    b, s, _ = xbc.shape
    seq = lambda n: pl.BlockSpec((1, L, n), lambda bi, ci: (bi, ci, 0))
    full = lambda a: pl.BlockSpec(a.shape, lambda bi, ci: (0,) * a.ndim)
    per_b = lambda a: pl.BlockSpec((1,) + a.shape[1:], lambda bi, ci: (bi,) + (0,) * (a.ndim - 1))
    return pl.pallas_call(
        functools.partial(_ssd_kernel, L=L),
        grid=(b, s // L),
        in_specs=[seq(CONV_DIM), seq(SSD_INNER), seq(DT_PAD), full(cw), full(cb), full(dtb), full(alog),
                  full(dsk_e), full(nw), full(expand), per_b(s0), per_b(c0)],
        out_specs=[seq(SSD_INNER), per_b(s0), per_b(c0)],
        out_shape=[jax.ShapeDtypeStruct((b, s, SSD_INNER), BF16),
                   jax.ShapeDtypeStruct(s0.shape, F32), jax.ShapeDtypeStruct(c0.shape, F32)],
        scratch_shapes=[pltpu.VMEM((L + CONV_HALO, CONV_DIM), F32), pltpu.VMEM((D_STATE, SSD_INNER), F32)],
        compiler_params=_cparams(("arbitrary", "arbitrary")),
        name="ssd_mixer",
    )(xbc, z, dt, cw, cb, dtb, alog, dsk_e, nw, expand, s0, c0)


def _mix_residual_norm(sb_ref, y_ref, x_ref, sbw_ref, w_ref, g_ref, nw_ref, sc_ref, sh_ref):
    sbn = _rms(sb_ref[...].astype(F32)) * sbw_ref[...]
    cat = jnp.concatenate([sbn.astype(BF16), y_ref[...]], axis=-1)
    mix = jnp.dot(cat, w_ref[...], preferred_element_type=F32)
    x1 = x_ref[...] + g_ref[0] * mix
    h2 = _rms(x1) * nw_ref[...]
    return x1, h2 * (1.0 + sc_ref[0]) + sh_ref[0]


def _outproj_kernel(sb_ref, y_ref, x_ref, sbw_ref, w_ref, g_ref, nw_ref, sc_ref, sh_ref, x1_ref, h2_ref):
    x1, h2 = _mix_residual_norm(sb_ref, y_ref, x_ref, sbw_ref, w_ref, g_ref, nw_ref, sc_ref, sh_ref)
    x1_ref[...] = x1
    h2_ref[...] = h2.astype(h2_ref.dtype)


R_E1, R_E2, R_W1, R_W2, R_RANK1, R_RANK2 = range(6)


def _outproj_router_kernel(sb_ref, y_ref, x_ref, sbw_ref, w_ref, g_ref, nw_ref, sc_ref, sh_ref,
                           wrh_ref, wrl_ref, br_ref, tri_ref,
                           x1_ref, h2_ref, gates_ref, route_ref, count_ref):
    x1, h2 = _mix_residual_norm(sb_ref, y_ref, x_ref, sbw_ref, w_ref, g_ref, nw_ref, sc_ref, sh_ref)
    x1_ref[...] = x1
    h2_ref[...] = h2.astype(h2_ref.dtype)

    @pl.when(pl.program_id(0) == 0)
    def _():
        count_ref[...] = jnp.zeros_like(count_ref)

    h_hi = h2.astype(BF16)
    h_lo = (h2 - h_hi.astype(F32)).astype(BF16)
    dot = lambda a, b: jnp.dot(a, b, preferred_element_type=F32)
    logits = dot(h_hi, wrh_ref[...]) + (dot(h_lo, wrh_ref[...]) + dot(h_hi, wrl_ref[...])) + br_ref[...]
    lane = lax.broadcasted_iota(jnp.int32, logits.shape, 1)
    m1 = jnp.max(logits, axis=-1, keepdims=True)
    i1 = jnp.min(jnp.where(logits == m1, lane, LANES), axis=-1, keepdims=True)
    rest = jnp.where(lane == i1, -jnp.inf, logits)
    m2 = jnp.max(rest, axis=-1, keepdims=True)
    i2 = jnp.min(jnp.where(rest == m2, lane, LANES), axis=-1, keepdims=True)
    w1 = 1.0 / (1.0 + jnp.exp(m2 - m1))
    w2 = 1.0 - w1
    hot1 = lane == i1
    hot2 = lane == i2
    gates_ref[...] = jnp.where(hot1, w1, 0.0) + jnp.where(hot2, w2, 0.0)
    hot = jnp.where(hot1 | hot2, 1.0, 0.0)
    before = dot(tri_ref[...], hot.astype(BF16)) + count_ref[...]
    rank1 = jnp.sum(jnp.where(hot1, before, 0.0), axis=-1, keepdims=True)
    rank2 = jnp.sum(jnp.where(hot2, before, 0.0), axis=-1, keepdims=True)
    count_ref[...] += jnp.sum(hot, axis=0, keepdims=True)
    rec = jnp.zeros(logits.shape, F32)
    for ln, val in ((R_E1, i1.astype(F32)), (R_E2, i2.astype(F32)), (R_W1, w1), (R_W2, w2),
                    (R_RANK1, rank1), (R_RANK2, rank2)):
        rec = jnp.where(lane == ln, val, rec)
    route_ref[...] = rec


def _out_proj(sb, y, x, sbw, w_out, g, nw, sc, sh, tm, rows_per_stream, h2_dtype, router=None):
    t = x.shape[0]
    mr = g.shape[1]
    row = lambda n: pl.BlockSpec((tm, n), lambda i: (i, 0))
    full = lambda a: pl.BlockSpec(a.shape, lambda i: (0,) * a.ndim)
    ms = _mod_spec(tm, mr, rows_per_stream)
    in_specs = [row(SB_WIDTH), row(SSD_INNER), row(D_MODEL), full(sbw), full(w_out), ms, full(nw), ms, ms]
    out_specs = [row(D_MODEL), row(D_MODEL)]
    out_shape = [jax.ShapeDtypeStruct((t, D_MODEL), F32), jax.ShapeDtypeStruct((t, D_MODEL), h2_dtype)]
    args = [sb, y, x, sbw, w_out, g, nw, sc, sh]
    body = _outproj_kernel
    if router is not None:
        body = _outproj_router_kernel
        args += list(router)
        in_specs += [full(a) for a in router]
        out_specs += [row(LANES), row(LANES), pl.BlockSpec((1, LANES), lambda i: (0, 0))]
        out_shape += [jax.ShapeDtypeStruct((t, LANES), F32), jax.ShapeDtypeStruct((t, LANES), F32),
                      jax.ShapeDtypeStruct((1, LANES), F32)]
    return pl.pallas_call(
        body,
        grid=(t // tm,),
        in_specs=in_specs,
        out_specs=out_specs,
        out_shape=out_shape,
        compiler_params=_cparams(("arbitrary",)),
        name="out_proj" if router is None else "out_proj_router",
    )(*args)


def _ffn_kernel(h_ref, wg_ref, wu_ref, wd_ref, x_ref, g_ref, *rest, n_exp):
    gates_ref = rest[0] if n_exp > 1 else None
    o_ref, acc_ref = rest[-2:]
    e = pl.program_id(1)
    j = pl.program_id(2)

    @pl.when((e == 0) & (j == 0))
    def _():
        acc_ref[...] = jnp.zeros_like(acc_ref)

    part = _swiglu_partial(h_ref[...], wg_ref, wu_ref, wd_ref)
    if n_exp > 1:
        lane = lax.broadcasted_iota(jnp.int32, gates_ref.shape, 1)
        part = part * jnp.sum(jnp.where(lane == e, gates_ref[...], 0.0), axis=-1, keepdims=True)
    acc_ref[...] += part

    @pl.when((e == n_exp - 1) & (j == pl.num_programs(2) - 1))
    def _():
        o_ref[...] = x_ref[...] + g_ref[0] * acc_ref[...]


def _ffn(h, wg, wu, wd, x, g, gates, tm, tf, rows_per_stream):
    t = x.shape[0]
    n_exp = wg.shape[0]
    mr = g.shape[1]
    if mr == 1:
        gspec = pl.BlockSpec((1, 1, D_MODEL), lambda i, e, j: ((i * tm) // rows_per_stream, 0, 0))
    else:
        gspec = pl.BlockSpec((1, tm, D_MODEL), lambda i, e, j: (i, 0, 0))
    row = lambda n: pl.BlockSpec((tm, n), lambda i, e, j: (i, 0))
    in_specs = [row(D_MODEL),
                pl.BlockSpec((1, D_MODEL, tf), lambda i, e, j: (e, 0, j)),
                pl.BlockSpec((1, D_MODEL, tf), lambda i, e, j: (e, 0, j)),
                pl.BlockSpec((1, tf, D_MODEL), lambda i, e, j: (e, j, 0)),
                row(D_MODEL), gspec]
    args = [h, wg, wu, wd, x, g]
    if n_exp > 1:
        in_specs.append(row(LANES))
        args.append(gates)
    return pl.pallas_call(
        functools.partial(_ffn_kernel, n_exp=n_exp),
        grid=(t // tm, n_exp, D_FF // tf),
        in_specs=in_specs,
        out_specs=row(D_MODEL),
        out_shape=jax.ShapeDtypeStruct((t, D_MODEL), F32),
        scratch_shapes=[pltpu.VMEM((tm, D_MODEL), F32)],
        compiler_params=_cparams(("arbitrary", "arbitrary", "arbitrary")),
        name="ffn_dense" if n_exp == 1 else "ffn_experts",
    )(*args)


EXPERT_TILE = 512


def _dispatch_kernel(pos_ref, h_hbm, zeros_hbm, xs_hbm, sem, *, tm):
    del zeros_hbm
    base = pl.program_id(0) * tm

    def copy(r, slot):
        return pltpu.make_async_copy(h_hbm.at[pl.ds(base + r, 1)], xs_hbm.at[pl.ds(pos_ref[0, 0, slot * tm + r], 1)], sem)

    def issue(r, carry):
        copy(r, 0).start()
        copy(r, 1).start()
        return carry

    def drain(r, carry):
        copy(r, 0).wait()
        copy(r, 1).wait()
        return carry

    lax.fori_loop(0, tm, issue, 0, unroll=8)
    lax.fori_loop(0, tm, drain, 0, unroll=8)


def _dispatch(h, pos, n_rows, tm):
    t = h.shape[0]
    return pl.pallas_call(
        functools.partial(_dispatch_kernel, tm=tm),
        grid=(t // tm,),
        in_specs=[pl.BlockSpec((1, 1, 2 * tm), lambda i: (i, 0, 0), memory_space=pltpu.SMEM),
                  pl.BlockSpec(memory_space=pl.ANY), pl.BlockSpec(memory_space=pl.ANY)],
        out_specs=pl.BlockSpec(memory_space=pl.ANY),
        out_shape=jax.ShapeDtypeStruct((n_rows, D_MODEL), F32),
        scratch_shapes=[pltpu.SemaphoreType.DMA(())],
        input_output_aliases={2: 0},
        compiler_params=_cparams(("arbitrary",)),
        name="moe_dispatch",
    )(pos, h, jnp.zeros((n_rows, D_MODEL), F32))


def _swiglu_partial(h, wg_ref, wu_ref, wd_ref):
    gt = jnp.dot(h, wg_ref[0], preferred_element_type=F32)
    up = jnp.dot(h, wu_ref[0], preferred_element_type=F32)
    act = (_silu(gt) * up).astype(BF16)
    return jnp.dot(act, wd_ref[0], preferred_element_type=F32)


def _grouped_ffn_kernel(tile_expert_ref, n_active_ref, x_ref, wg_ref, wu_ref, wd_ref, o_ref):
    del tile_expert_ref
    j = pl.program_id(1)
    active = pl.program_id(0) < n_active_ref[0]

    @pl.when(active & (j == 0))
    def _():
        o_ref[...] = _swiglu_partial(x_ref[...].astype(BF16), wg_ref, wu_ref, wd_ref)

    @pl.when(active & (j > 0))
    def _():
        o_ref[...] += _swiglu_partial(x_ref[...].astype(BF16), wg_ref, wu_ref, wd_ref)

    @pl.when(jnp.logical_not(active) & (j == 0))
    def _():
        o_ref[...] = jnp.zeros_like(o_ref)


def _grouped_ffn(xs, tile_expert, n_active, wg, wu, wd, tf):
    n_rows = xs.shape[0]
    row = pl.BlockSpec((EXPERT_TILE, D_MODEL), lambda i, j, te, na: (i, 0))
    grid_spec = pltpu.PrefetchScalarGridSpec(
        num_scalar_prefetch=2,
        grid=(n_rows // EXPERT_TILE, D_FF // tf),
        in_specs=[row,
                  pl.BlockSpec((1, D_MODEL, tf), lambda i, j, te, na: (te[i], 0, j)),
                  pl.BlockSpec((1, D_MODEL, tf), lambda i, j, te, na: (te[i], 0, j)),
                  pl.BlockSpec((1, tf, D_MODEL), lambda i, j, te, na: (te[i], j, 0))],
        out_specs=row)
    return pl.pallas_call(
        _grouped_ffn_kernel,
        grid_spec=grid_spec,
        out_shape=jax.ShapeDtypeStruct((n_rows, D_MODEL), F32),
        compiler_params=_cparams(("arbitrary", "arbitrary")),
        name="ffn_grouped",
    )(tile_expert, n_active, xs, wg, wu, wd)


def _combine_kernel(pos_ref, nxt_ref, route_ref, x_ref, g_ref, ye_hbm, o_ref, buf_ref, sem_ref, *, tm):
    i = pl.program_id(0)
    n = pl.num_programs(0)
    slot = i % 2

    def copy(idx_ref, r, which, s):
        return pltpu.make_async_copy(ye_hbm.at[pl.ds(idx_ref[0, 0, which * tm + r], 1)],
                                     buf_ref.at[s, pl.ds(which * tm + r, 1)], sem_ref.at[s])

    def fetch(idx_ref, s):
        def issue(r, carry):
            copy(idx_ref, r, 0, s).start()
            copy(idx_ref, r, 1, s).start()
            return carry
        lax.fori_loop(0, tm, issue, 0, unroll=8)

    @pl.when(i == 0)
    def _():
        fetch(pos_ref, 0)

    @pl.when(i + 1 < n)
    def _():
        fetch(nxt_ref, 1 - slot)

    def drain(r, carry):
        copy(pos_ref, r, 0, slot).wait()
        copy(pos_ref, r, 1, slot).wait()
        return carry

    lax.fori_loop(0, tm, drain, 0, unroll=8)
    rec = route_ref[...]
    w1 = rec[:, R_W1:R_W1 + 1]
    w2 = rec[:, R_W2:R_W2 + 1]
    rows = buf_ref[slot]
    o_ref[...] = x_ref[...] + g_ref[0] * (w1 * rows[0:tm] + w2 * rows[tm:2 * tm])


def _combine(ye, pos, route, x, g, tm, rows_per_stream):
    t = x.shape[0]
    n = t // tm
    row = lambda w: pl.BlockSpec((tm, w), lambda i: (i, 0))
    smem = lambda f: pl.BlockSpec((1, 1, 2 * tm), f, memory_space=pltpu.SMEM)
    return pl.pallas_call(
        functools.partial(_combine_kernel, tm=tm),
        grid=(n,),
        in_specs=[smem(lambda i: (i, 0, 0)), smem(lambda i: (jnp.minimum(i + 1, n - 1), 0, 0)),
                  row(LANES), row(D_MODEL), _mod_spec(tm, g.shape[1], rows_per_stream),
                  pl.BlockSpec(memory_space=pl.ANY)],
        out_specs=row(D_MODEL),
        out_shape=jax.ShapeDtypeStruct((t, D_MODEL), F32),
        scratch_shapes=[pltpu.VMEM((2, 2 * tm, D_MODEL), F32), pltpu.SemaphoreType.DMA((2,))],
        compiler_params=_cparams(("arbitrary",)),
        name="moe_combine",
    )(pos, pos, route, x, g, ye)


def _sorted_experts(h2f, route, counts, x1, g_f, wg, wu, wd, tm, rows_per_stream):
    t = h2f.shape[0]
    n_tiles = (2 * t) // EXPERT_TILE + N_EXPERTS
    cnt = counts[0, :N_EXPERTS].astype(jnp.int32)
    padded = ((cnt + EXPERT_TILE - 1) // EXPERT_TILE) * EXPERT_TILE
    ends = jnp.cumsum(padded)
    starts = ends - padded
    experts = jnp.arange(N_EXPERTS, dtype=jnp.int32)

    def slots(e_lane, r_lane):
        e = route[:, e_lane].astype(jnp.int32)
        start = jnp.sum(jnp.where(e[:, None] == experts[None, :], starts[None, :], 0), axis=-1)
        return start + route[:, r_lane].astype(jnp.int32)

    pos = jnp.stack([slots(R_E1, R_RANK1).reshape(t // tm, tm), slots(R_E2, R_RANK2).reshape(t // tm, tm)], axis=1)
    pos = pos.reshape(t // tm, 1, 2 * tm)
    tile_start = jnp.arange(n_tiles, dtype=jnp.int32) * EXPERT_TILE
    n_active = (ends[-1] // EXPERT_TILE).astype(jnp.int32)
    clipped = jnp.minimum(tile_start, ends[-1] - EXPERT_TILE)
    tile_expert = jnp.sum((clipped[:, None] >= ends[None, :]).astype(jnp.int32), axis=-1)
    xs = _dispatch(h2f, pos, n_tiles * EXPERT_TILE, tm)
    ye = _grouped_ffn(xs, tile_expert, n_active.reshape(1), wg, wu, wd, D_FF // 2)
    return _combine(ye, pos, route, x1, g_f, tm, rows_per_stream)


def _layer(x, mod, l, p, consts, k_past, v_past, s0, c0, tm, ssd_chunk, attn_blk, expand_mod, sorted_experts):
    b, s, _ = x.shape
    t = b * s

    def mod_rows(i):
        m = mod[:, i, :]
        if expand_mod:
            return jnp.repeat(m, s, axis=0).reshape(t // tm, tm, D_MODEL)
        return m.reshape(b, 1, D_MODEL)

    sh_m, sc_m, g_m, sh_f, sc_f, g_f = [mod_rows(i) for i in range(6)]
    xf = x.reshape(t, D_MODEL)
    q, k, v, kf, vf, z, xbc, dt = _in_proj(xf, sc_m, sh_m, p['norm_mix_w'][l], p['w_in'][l], consts['head_ones'],
                                           p['q_norm_w'][l], p['k_norm_w'][l], tm, s)
    shp = lambda a: a.reshape(b, s, a.shape[-1])
    if k_past is None:
        sb = _attn_prompt(shp(q), shp(k), shp(v), consts['from_ones'], attn_blk)
    else:
        sb = _attn_sample(shp(q), shp(k), shp(v), k_past, v_past, l, consts['from_ones'], attn_blk)
    y, s_new, c_new = _ssd(shp(xbc), shp(z), shp(dt), p['conv_w'][l], p['conv_b'][l], p['dt_bias'][l],
                           p['a_log'][l], p['d_skip'][l], p['ssd_norm_w'][l], consts['head_expand'],
                           s0, c0, ssd_chunk)
    mix_args = (sb.reshape(t, SB_WIDTH), y.reshape(t, SSD_INNER), xf, p['sb_norm_w'][l], p['w_out'][l], g_m,
                p['norm_ffn_w'][l], sc_f, sh_f, tm, s)
    i = l // 2
    tf = D_FF // 2
    if l % 2 == 0:
        x1, h2 = _out_proj(*mix_args, BF16)
        x2 = _ffn(h2, p['w_gate_dense'][i:i + 1], p['w_up_dense'][i:i + 1], p['w_down_dense'][i:i + 1],
                  x1, g_f, None, tm, tf, s)
    else:
        router = (p['w_router_hi'][i], p['w_router_lo'][i], p['b_router'][i], consts['before_ones'][:tm, :tm])
        experts = (p['w_gate_moe'][i], p['w_up_moe'][i], p['w_down_moe'][i])
        if sorted_experts:
            x1, h2, _, route, counts = _out_proj(*mix_args, F32, router)
            x2 = _sorted_experts(h2, route, counts, x1, g_f, *experts, tm // 2, s)
        else:
            x1, h2, gates, _, _ = _out_proj(*mix_args, BF16, router)
            x2 = _ffn(h2, *experts, x1, g_f, gates, tm, tf, s)
    return x2.reshape(b, s, D_MODEL), kf, vf, s_new, c_new


def kernel(x_prompt, x_sample, c_prompt, c_sample, cache_sb_k, cache_sb_v, state_ssd, state_conv, w_mod, b_mod, norm_mix_w, norm_ffn_w, w_in, q_norm_w, k_norm_w, sb_norm_w, conv_w, conv_b, dt_bias, a_log, d_skip, ssd_norm_w, w_out, w_gate_dense, w_up_dense, w_down_dense, w_router, b_router, w_gate_moe, w_up_moe, w_down_moe):
    depth = w_in.shape[0]
    bp, sp, _ = x_prompt.shape
    bs, ss, _ = x_sample.shape
    past = cache_sb_k.shape[2]
    n_moe = w_router.shape[0]

    row = lambda a: a.reshape(depth, 1, a.shape[-1])
    lane_pad = lambda a: jnp.pad(a, ((0, 0), (0, LANES - a.shape[-1]))).reshape(a.shape[0], 1, LANES)
    w_router_pad = jnp.pad(w_router, ((0, 0), (0, 0), (0, LANES - N_EXPERTS)))
    p = {
        'norm_mix_w': row(norm_mix_w), 'norm_ffn_w': row(norm_ffn_w),
        'w_in': jnp.pad(w_in, ((0, 0), (0, 0), (0, IN_MAIN + DT_PAD - w_in.shape[-1]))).astype(BF16),
        'q_norm_w': row(jnp.tile(q_norm_w, (1, N_HEADS))), 'k_norm_w': row(jnp.tile(k_norm_w, (1, N_HEADS))),
        'sb_norm_w': row(sb_norm_w), 'conv_w': conv_w, 'conv_b': row(conv_b),
        'dt_bias': lane_pad(dt_bias), 'a_log': lane_pad(a_log),
        'd_skip': row(jnp.repeat(d_skip, HEAD_DIM, axis=-1)), 'ssd_norm_w': row(ssd_norm_w),
        'w_out': w_out.astype(BF16),
        'w_gate_dense': w_gate_dense.astype(BF16), 'w_up_dense': w_up_dense.astype(BF16),
        'w_down_dense': w_down_dense.astype(BF16),
        'w_router_hi': w_router_pad.astype(BF16),
        'w_router_lo': (w_router_pad - w_router_pad.astype(BF16).astype(F32)).astype(BF16),
        'b_router': jnp.pad(b_router, ((0, 0), (0, LANES - N_EXPERTS)), constant_values=-1e30).reshape(n_moe, 1, LANES),
        'w_gate_moe': w_gate_moe.astype(BF16), 'w_up_moe': w_up_moe.astype(BF16),
        'w_down_moe': w_down_moe.astype(BF16),
    }
    attn_blk = 256
    tm_prompt = 512
    idx = jnp.arange(attn_blk, dtype=jnp.int32)
    wide = jnp.arange(SB_WIDTH, dtype=jnp.int32)
    tok = jnp.arange(tm_prompt, dtype=jnp.int32)
    consts = {
        'from_ones': (idx[:, None] >= idx[None, :]).astype(BF16),
        'before_ones': (tok[None, :] < tok[:, None]).astype(BF16),
        'head_ones': (wide[:, None] // HEAD_DIM == wide[None, :] // HEAD_DIM).astype(BF16),
        'head_expand': (jnp.arange(LANES, dtype=jnp.int32)[:, None] == wide[None, :] // HEAD_DIM).astype(BF16),
    }

    c_all = jnp.concatenate([c_prompt, c_sample], axis=0)
    mod = _modulation(c_all, w_mod, b_mod).reshape(depth, bp + bs, 6, D_MODEL)

    kp = cache_sb_k.reshape(depth * bs, past, SB_WIDTH)
    vp = cache_sb_v.reshape(depth * bs, past, SB_WIDTH)
    halo_pad = ((0, 0), (0, 0), (CONV_HALO - (CONV_WIDTH - 1), 0), (0, 0))
    conv_in = jnp.pad(state_conv, halo_pad)
    ssd_in = state_ssd.reshape(depth, bs, N_HEADS * HEAD_DIM, D_STATE)
    zero_s = jnp.zeros((bp, N_HEADS * HEAD_DIM, D_STATE), F32)
    zero_c = jnp.zeros((bp, CONV_HALO, CONV_DIM), F32)

    xp, xs = x_prompt, x_sample
    outs_p, outs_s = [], []
    for l in range(depth):
        xp, k, v, s, c = _layer(xp, mod[l, :bp], l, p, consts, None, None, zero_s, zero_c,
                                tm=tm_prompt, ssd_chunk=256, attn_blk=attn_blk, expand_mod=False, sorted_experts=True)
        outs_p.append((k.reshape(bp, sp, N_HEADS, HEAD_DIM), v.reshape(bp, sp, N_HEADS, HEAD_DIM),
                       s.reshape(bp, N_HEADS, HEAD_DIM, D_STATE), c[:, CONV_HALO - (CONV_WIDTH - 1):]))
        xs, k, v, s, c = _layer(xs, mod[l, bp:], l, p, consts, kp, vp, ssd_in[l], conv_in[l],
                                tm=bs * ss, ssd_chunk=ss, attn_blk=attn_blk, expand_mod=True, sorted_experts=False)
        outs_s.append((k.reshape(bs, ss, N_HEADS, HEAD_DIM), v.reshape(bs, ss, N_HEADS, HEAD_DIM),
                       s.reshape(bs, N_HEADS, HEAD_DIM, D_STATE), c[:, CONV_HALO - (CONV_WIDTH - 1):]))
    stack = lambda outs, i: jnp.stack([o[i] for o in outs])
    return (xp, xs,
            stack(outs_p, 0), stack(outs_p, 1), stack(outs_p, 2), stack(outs_p, 3),
            stack(outs_s, 0), stack(outs_s, 1), stack(outs_s, 2), stack(outs_s, 3))
```

```python
import functools

import jax
import jax.numpy as jnp
from jax import lax
from jax.experimental import pallas as pl
from jax.experimental.pallas import tpu as pltpu

F32 = jnp.float32
BF16 = jnp.bfloat16
HIGHEST = lax.Precision.HIGHEST

EPS = 1e-6
LOG2_E = 1.4426950408889634
D_MODEL = 1024
N_HEADS = 8
HEAD_DIM = 64
SB_WIDTH = 512
SSD_INNER = 512
SSD_GROUPS = 2
D_STATE = 128
CONV_WIDTH = 4
CONV_DIM = 1024
IN_MAIN = 3072
DT_PAD = 128
D_FF = 2816
N_EXPERTS = 8
LANES = 128
CONV_HALO = 8

VMEM_LIMIT = 56 * 1024 * 1024


def _cparams(sem, vmem=VMEM_LIMIT):
    return pltpu.CompilerParams(dimension_semantics=sem, vmem_limit_bytes=vmem)


def _silu(x):
    return x * (1.0 / (1.0 + jnp.exp2(x * -LOG2_E)))


def _softplus(x):
    return jnp.maximum(x, 0.0) + jnp.log(1.0 + jnp.exp(-jnp.abs(x)))


def _rms(x):
    return x * lax.rsqrt(jnp.mean(x * x, axis=-1, keepdims=True) + EPS)


def _mod_kernel(c_ref, w_ref, b_ref, o_ref):
    s = _silu(c_ref[...])
    o_ref[0] = jnp.dot(s, w_ref[0], preferred_element_type=F32, precision=HIGHEST) + b_ref[0]


def _modulation(c_all, w_mod, b_mod):
    depth, _, n6 = w_mod.shape
    ns = c_all.shape[0]
    nj = n6 // D_MODEL
    return pl.pallas_call(
        _mod_kernel,
        grid=(depth, nj),
        in_specs=[pl.BlockSpec((ns, D_MODEL), lambda l, j: (0, 0)),
                  pl.BlockSpec((1, D_MODEL, D_MODEL), lambda l, j: (l, 0, j)),
                  pl.BlockSpec((1, 1, D_MODEL), lambda l, j: (l, 0, j))],
        out_specs=pl.BlockSpec((1, ns, D_MODEL), lambda l, j: (l, 0, j)),
        out_shape=jax.ShapeDtypeStruct((depth, ns, n6), F32),
        compiler_params=_cparams(("arbitrary", "arbitrary")),
        name="adaln_mod",
    )(c_all, w_mod, b_mod.reshape(depth, 1, n6))


def _inproj_kernel(x_ref, sc_ref, sh_ref, nw_ref, w_ref, bd_ref, qw_ref, kw_ref, *rest, nt):
    kf_ref, vf_ref = rest[-5:-3]

    @pl.when(pl.program_id(0) < nt)
    def _():
        _inproj_tile(x_ref, sc_ref, sh_ref, nw_ref, w_ref, bd_ref, qw_ref, kw_ref, *rest[-8:])

    @pl.when(pl.program_id(0) >= nt)
    def _():
        kf_ref[...] = jnp.zeros_like(kf_ref)
        vf_ref[...] = jnp.zeros_like(vf_ref)


def _inproj_tile(x_ref, sc_ref, sh_ref, nw_ref, w_ref, bd_ref, qw_ref, kw_ref,
                 q_ref, k_ref, v_ref, kf_ref, vf_ref, z_ref, xbc_ref, dt_ref):
    h = _rms(x_ref[...]) * nw_ref[...]
    h = h * (1.0 + sc_ref[0]) + sh_ref[0]
    proj = jnp.dot(h.astype(BF16), w_ref[...], preferred_element_type=F32)

    def head_norm(t, w):
        ss = jnp.dot((t * t).astype(BF16), bd_ref[...], preferred_element_type=F32)
        return t * lax.rsqrt(ss * (1.0 / HEAD_DIM) + EPS) * w

    q = head_norm(proj[:, 0:SB_WIDTH], qw_ref[...])
    k = head_norm(proj[:, SB_WIDTH:2 * SB_WIDTH], kw_ref[...])
    v = proj[:, 2 * SB_WIDTH:3 * SB_WIDTH]
    q_ref[...] = (q * (HEAD_DIM ** -0.5 * LOG2_E)).astype(BF16)
    k_ref[...] = k.astype(BF16)
    v_ref[...] = v.astype(BF16)
    for hd in range(N_HEADS):
        cols = slice(hd * HEAD_DIM, (hd + 1) * HEAD_DIM)
        kf_ref[pl.ds(hd, k.shape[0], stride=N_HEADS), :] = k[:, cols]
        vf_ref[pl.ds(hd, v.shape[0], stride=N_HEADS), :] = v[:, cols]
    z_ref[...] = proj[:, 3 * SB_WIDTH:3 * SB_WIDTH + SSD_INNER].astype(BF16)
    xbc_ref[...] = proj[:, 3 * SB_WIDTH + SSD_INNER:IN_MAIN]
    dt_ref[...] = proj[:, IN_MAIN:IN_MAIN + DT_PAD]


def _mod_spec(tm, mr, rows_per_stream, tile=lambda i: i):
    if mr == 1:
        return pl.BlockSpec((1, 1, D_MODEL), lambda i: ((tile(i) * tm) // rows_per_stream, 0, 0))
    return pl.BlockSpec((1, tm, D_MODEL), lambda i: (tile(i), 0, 0))


def _in_proj(x, sc, sh, nw, w_pad, bd, qw, kw, tm, rows_per_stream, layer, depth, kv_states):
    t = x.shape[0]
    mr = sc.shape[1]
    nt = t // tm
    first = kv_states is None
    assert first == (layer == 0)
    tile = lambda i: jnp.minimum(i, nt - 1)
    row = lambda n: pl.BlockSpec((tm, n), lambda i: (tile(i), 0))
    full = lambda a: pl.BlockSpec(a.shape, lambda i: (0,) * a.ndim)
    state = pl.BlockSpec((tm * N_HEADS, HEAD_DIM), lambda i: (layer * nt + i, 0))
    state_shape = jax.ShapeDtypeStruct((depth * t * N_HEADS, HEAD_DIM), F32)
    outs = [(SB_WIDTH, BF16), (SB_WIDTH, BF16), (SB_WIDTH, BF16), None, None,
            (SSD_INNER, BF16), (CONV_DIM, F32), (DT_PAD, F32)]
    in_specs = [row(D_MODEL), _mod_spec(tm, mr, rows_per_stream, tile), _mod_spec(tm, mr, rows_per_stream, tile),
                full(nw), full(w_pad), full(bd), full(qw), full(kw)]
    args = [x, sc, sh, nw, w_pad, bd, qw, kw]
    aliases = {}
    if not first:
        in_specs += [pl.BlockSpec(memory_space=pl.ANY)] * 2
        aliases = {len(args): 3, len(args) + 1: 4}
        args += list(kv_states)
    return pl.pallas_call(
        functools.partial(_inproj_kernel, nt=nt),
        grid=(depth * nt if first else nt,),
        in_specs=in_specs,
        out_specs=[state if o is None else row(o[0]) for o in outs],
        out_shape=[state_shape if o is None else jax.ShapeDtypeStruct((t, o[0]), o[1]) for o in outs],
        input_output_aliases=aliases,
        compiler_params=_cparams(("arbitrary",)),
        name="in_proj",
    )(*args)


def _sb_blocks(qs, blocks, from_ones, acc_ref, c_ref):
    acc = acc_ref[...]
    c = c_ref[...]
    for kblk, vblk, mask in blocks:
        z = lax.dot_general(qs, kblk, (((1,), (1,)), ((), ())), preferred_element_type=F32)
        u = jnp.maximum(z, 0.0) + jnp.log(1.0 + jnp.exp2(-jnp.abs(z))) * LOG2_E
        if mask is not None:
            u = jnp.where(mask, u, 0.0)
        cum = jnp.dot(u.astype(BF16), from_ones, preferred_element_type=F32)
        a = jnp.exp2(z - cum)
        if mask is not None:
            a = jnp.where(mask, a, 0.0)
        pv = jnp.dot(a.astype(BF16), vblk, preferred_element_type=F32)
        acc = acc + jnp.exp2(-c) * pv
        c = c + cum[:, 0:1]
    acc_ref[...] = acc
    c_ref[...] = c


def _stack_heads(q):
    lane = lax.broadcasted_iota(jnp.int32, q.shape, 1)
    zero = jnp.zeros_like(q)
    return jnp.concatenate([jnp.where(lane < HEAD_DIM, q, zero), jnp.where(lane >= HEAD_DIM, q, zero)], axis=0)


def _unstack_heads(acc, tq):
    lane = lax.broadcasted_iota(jnp.int32, (tq, LANES), 1)
    return jnp.where(lane < HEAD_DIM, acc[0:tq], acc[tq:2 * tq])


def _strict_lower(n, m):
    r = lax.broadcasted_iota(jnp.int32, (n, m), 0)
    c = lax.broadcasted_iota(jnp.int32, (n, m), 1)
    return c < r


def _stacked_causal(tq, tk):
    r = lax.broadcasted_iota(jnp.int32, (2 * tq, tk), 0)
    c = lax.broadcasted_iota(jnp.int32, (2 * tq, tk), 1)
    return c < jnp.where(r >= tq, r - tq, r)


def _attn_prompt_kernel(q_ref, k_ref, v_ref, m_ref, o_ref, acc_ref, c_ref, *, blk):
    i = pl.program_id(2)
    qs = _stack_heads(q_ref[0])
    acc_ref[...] = jnp.zeros_like(acc_ref)
    c_ref[...] = jnp.zeros_like(c_ref)
    from_ones = m_ref[...]
    diag_mask = _stacked_causal(blk, blk)

    def kv(kb, mask=None):
        start = pl.multiple_of(kb * blk, blk)
        return k_ref[0, pl.ds(start, blk), :], v_ref[0, pl.ds(start, blk), :], mask

    def run(top, n):
        _sb_blocks(qs, [kv(top - d) for d in range(n)], from_ones, acc_ref, c_ref)

    @pl.when(i == 0)
    def _():
        _sb_blocks(qs, [kv(i, diag_mask)], from_ones, acc_ref, c_ref)

    @pl.when(i > 0)
    def _():
        _sb_blocks(qs, [kv(i, diag_mask), kv(i - 1)], from_ones, acc_ref, c_ref)

    rest = jnp.maximum(i - 1, 0)
    n4 = rest // 4

    def body(j, carry):
        run(rest - 1 - 4 * j, 4)
        return carry

    lax.fori_loop(0, n4, body, 0)
    tail = rest - 4 * n4

    @pl.when(tail >= 2)
    def _():
        run(tail - 1, 2)

    @pl.when(tail % 2 == 1)
    def _():
        run(0, 1)

    o_ref[0] = _unstack_heads(acc_ref[...], blk).astype(o_ref.dtype)


def _attn_prompt(q, k, v, from_ones, blk):
    b, s, _ = q.shape
    npair = SB_WIDTH // LANES
    qspec = pl.BlockSpec((1, blk, LANES), lambda bi, p, i: (bi, i, p))
    kvspec = pl.BlockSpec((1, s, LANES), lambda bi, p, i: (bi, 0, p))
    return pl.pallas_call(
        functools.partial(_attn_prompt_kernel, blk=blk),
        grid=(b, npair, s // blk),
        in_specs=[qspec, kvspec, kvspec, pl.BlockSpec(from_ones.shape, lambda bi, p, i: (0, 0))],
        out_specs=qspec,
        out_shape=jax.ShapeDtypeStruct((b, s, SB_WIDTH), BF16),
        scratch_shapes=[pltpu.VMEM((2 * blk, LANES), F32), pltpu.VMEM((2 * blk, 1), F32)],
        compiler_params=_cparams(("arbitrary", "arbitrary", "arbitrary")),
        name="sb_attn_prompt",
    )(q, k, v, from_ones)


PAST_BLOCKS_PER_STEP = 2


def _attn_sample_kernel(q_ref, kn_ref, vn_ref, kp_ref, vp_ref, m_ref, o_ref, acc_ref, c_ref, *, blk):
    j = pl.program_id(1)
    tq = q_ref.shape[1]
    npair = SB_WIDTH // LANES
    from_ones = m_ref[...]
    pair_cols = lambda p: slice(p * LANES, (p + 1) * LANES)
    queries = lambda p: _stack_heads(q_ref[0, :, pair_cols(p)])

    @pl.when(j == 0)
    def _():
        acc_ref[...] = jnp.zeros_like(acc_ref)
        c_ref[...] = jnp.zeros_like(c_ref)
        mask = _stacked_causal(tq, blk)
        for p in range(npair):
            block = (kn_ref[0, :, pair_cols(p)], vn_ref[0, :, pair_cols(p)], mask)
            _sb_blocks(queries(p), [block], from_ones, acc_ref.at[p], c_ref.at[p])

    @pl.when(j > 0)
    def _():
        def pair_block(ref, sub, p):
            head = lambda hd: ref[pl.ds(sub * blk * N_HEADS + hd, blk, stride=N_HEADS), :]
            return jnp.concatenate([head(2 * p), head(2 * p + 1)], axis=-1).astype(BF16)

        for p in range(npair):
            blocks = [(pair_block(kp_ref, sub, p), pair_block(vp_ref, sub, p), None)
                      for sub in reversed(range(PAST_BLOCKS_PER_STEP))]
            _sb_blocks(queries(p), blocks, from_ones, acc_ref.at[p], c_ref.at[p])

    @pl.when(j == pl.num_programs(1) - 1)
    def _():
        for p in range(npair):
            o_ref[0, :, pair_cols(p)] = _unstack_heads(acc_ref[p], tq).astype(o_ref.dtype)


def _attn_sample(q, k_new, v_new, k_past, v_past, past, layer, from_ones, blk):
    b, tq, _ = q.shape
    npair = SB_WIDTH // LANES
    span = PAST_BLOCKS_PER_STEP * blk
    nsteps = past // span
    assert nsteps * span == past
    nspec = pl.BlockSpec((1, tq, SB_WIDTH), lambda bi, j: (bi, 0, 0))
    kspec = pl.BlockSpec((1, blk, SB_WIDTH), lambda bi, j: (bi, 0, 0))
    pspec = pl.BlockSpec((span * N_HEADS, HEAD_DIM),
                         lambda bi, j: ((layer * b + bi) * nsteps + nsteps - jnp.maximum(j, 1), 0))
    pad = ((0, 0), (0, blk - tq), (0, 0))
    k_new, v_new = jnp.pad(k_new, pad), jnp.pad(v_new, pad)
    return pl.pallas_call(
        functools.partial(_attn_sample_kernel, blk=blk),
        grid=(b, 1 + nsteps),
        in_specs=[nspec, kspec, kspec, pspec, pspec, pl.BlockSpec(from_ones.shape, lambda bi, j: (0, 0))],
        out_specs=nspec,
        out_shape=jax.ShapeDtypeStruct((b, tq, SB_WIDTH), BF16),
        scratch_shapes=[pltpu.VMEM((npair, 2 * tq, LANES), F32), pltpu.VMEM((npair, 2 * tq, 1), F32)],
        compiler_params=_cparams(("arbitrary", "arbitrary")),
        name="sb_attn_sample",
    )(q, k_new, v_new, k_past, v_past, from_ones)


def _ssd_kernel(xbc_ref, z_ref, dt_ref, cw_ref, cb_ref, dtb_ref, alog_ref, dsk_ref, nw_ref, exp_ref,
                s0_ref, c0_ref, y_ref, sT_out_ref, cs_out_ref, ext_ref, st_ref, *, L):
    ci = pl.program_id(1)
    nc = pl.num_programs(1)
    gw = SSD_INNER // SSD_GROUPS
    hpg = N_HEADS // SSD_GROUPS

    @pl.when(ci == 0)
    def _():
        ext_ref[0:CONV_HALO, :] = c0_ref[0]
        st_ref[...] = s0_ref[0].T

    ext_ref[CONV_HALO:CONV_HALO + L, :] = xbc_ref[0]
    conv = cb_ref[...]
    for w in range(CONV_WIDTH):
        off = CONV_HALO - (CONV_WIDTH - 1) + w
        conv = conv + ext_ref[off:off + L, :] * cw_ref[w:w + 1, :]
    xbc = _silu(conv)
    tail = ext_ref[L:L + CONV_HALO, :]
    ext_ref[0:CONV_HALO, :] = tail

    x = xbc[:, 0:SSD_INNER]
    bm = xbc[:, SSD_INNER:SSD_INNER + SSD_GROUPS * D_STATE]
    cm = xbc[:, SSD_INNER + SSD_GROUPS * D_STATE:]
    dt = _softplus(dt_ref[0] + dtb_ref[...])
    da = dt * (-jnp.exp(alog_ref[...]))
    tril = jnp.where(_strict_lower(L, L) | (lax.broadcasted_iota(jnp.int32, (L, L), 0)
                                            == lax.broadcasted_iota(jnp.int32, (L, L), 1)), 1.0, 0.0)
    a_cum = jnp.dot(tril, da * LOG2_E, preferred_element_type=F32, precision=HIGHEST)
    a_cum_t = a_cum.T
    a_last = a_cum[L - 1:L, :]
    expand = exp_ref[...]

    def bcast(t):
        hi = t.astype(BF16)
        lo = (t - hi.astype(F32)).astype(BF16)
        return (jnp.dot(hi, expand, preferred_element_type=F32) + jnp.dot(lo, expand, preferred_element_type=F32))

    per_head = bcast(jnp.concatenate([dt, jnp.exp2(a_cum), jnp.exp2(a_last - a_cum)], axis=0))
    dt_e, ea_e, ds_e = per_head[0:L], per_head[L:2 * L], per_head[2 * L:3 * L]
    xdt = x * dt_e
    wgt = (xdt * ds_e).astype(BF16)
    xdt_b = xdt.astype(BF16)
    bm_t = bm.T.astype(BF16)
    bm_b = bm.astype(BF16)
    cm_b = cm.astype(BF16)
    causal = _strict_lower(L, L) | (lax.broadcasted_iota(jnp.int32, (L, L), 0)
                                    == lax.broadcasted_iota(jnp.int32, (L, L), 1))
    lane_g = lax.broadcasted_iota(jnp.int32, (L, gw), 1)
    y_parts = []
    for g in range(SSD_GROUPS):
        gs = slice(g * gw, (g + 1) * gw)
        ns = slice(g * D_STATE, (g + 1) * D_STATE)
        cb = lax.dot_general(cm_b[:, ns], bm_b[:, ns], (((1,), (1,)), ((), ())), preferred_element_type=F32)
        st_g = st_ref[:, gs]
        y_g = jnp.dot(cm_b[:, ns], st_g.astype(BF16), preferred_element_type=F32) * ea_e[:, gs]
        for hl in range(hpg):
            hd = g * hpg + hl
            seg = a_cum[:, hd:hd + 1] - a_cum_t[hd:hd + 1, :]
            decay = jnp.where(causal, jnp.exp2(seg), 0.0)
            scores = (cb * decay).astype(BF16)
            xh = jnp.where((lane_g >= hl * HEAD_DIM) & (lane_g < (hl + 1) * HEAD_DIM), xdt_b[:, gs],
                           jnp.zeros_like(xdt_b[:, gs]))
            y_g = y_g + jnp.dot(scores, xh, preferred_element_type=F32)
        y_parts.append(y_g)
        new_states = jnp.dot(bm_t[ns, :], wgt[:, gs], preferred_element_type=F32)
        st_ref[:, gs] = st_g * ea_e[L - 1:L, gs] + new_states
    y = jnp.concatenate(y_parts, axis=-1) + dsk_ref[...] * x
    y = y * _silu(z_ref[0].astype(F32))
    y = jnp.concatenate([_rms(y[:, g * gw:(g + 1) * gw]) for g in range(SSD_GROUPS)], axis=-1) * nw_ref[...]
    y_ref[0] = y.astype(y_ref.dtype)

    @pl.when(ci == nc - 1)
    def _():
        sT_out_ref[0] = st_ref[...].T
        cs_out_ref[0] = tail


def _ssd(xbc, z, dt, cw, cb, dtb, alog, dsk_e, nw, expand, s0, c0, L):
    b, s, _ = xbc.shape
    seq = lambda n: pl.BlockSpec((1, L, n), lambda bi, ci: (bi, ci, 0))
    full = lambda a: pl.BlockSpec(a.shape, lambda bi, ci: (0,) * a.ndim)
    per_b = lambda a: pl.BlockSpec((1,) + a.shape[1:], lambda bi, ci: (bi,) + (0,) * (a.ndim - 1))
    return pl.pallas_call(
        functools.partial(_ssd_kernel, L=L),
        grid=(b, s // L),
        in_specs=[seq(CONV_DIM), seq(SSD_INNER), seq(DT_PAD), full(cw), full(cb), full(dtb), full(alog),
                  full(dsk_e), full(nw), full(expand), per_b(s0), per_b(c0)],
        out_specs=[seq(SSD_INNER), per_b(s0), per_b(c0)],
        out_shape=[jax.ShapeDtypeStruct((b, s, SSD_INNER), BF16),
                   jax.ShapeDtypeStruct(s0.shape, F32), jax.ShapeDtypeStruct(c0.shape, F32)],
        scratch_shapes=[pltpu.VMEM((L + CONV_HALO, CONV_DIM), F32), pltpu.VMEM((D_STATE, SSD_INNER), F32)],
        compiler_params=_cparams(("arbitrary", "arbitrary")),
        name="ssd_mixer",
    )(xbc, z, dt, cw, cb, dtb, alog, dsk_e, nw, expand, s0, c0)


def _mix_residual_norm(sb_ref, y_ref, x_ref, sbw_ref, w_ref, g_ref, nw_ref, sc_ref, sh_ref):
    sbn = _rms(sb_ref[...].astype(F32)) * sbw_ref[...]
    cat = jnp.concatenate([sbn.astype(BF16), y_ref[...]], axis=-1)
    mix = jnp.dot(cat, w_ref[...], preferred_element_type=F32)
    x1 = x_ref[...] + g_ref[0] * mix
    h2 = _rms(x1) * nw_ref[...]
    return x1, h2 * (1.0 + sc_ref[0]) + sh_ref[0]


def _outproj_kernel(sb_ref, y_ref, x_ref, sbw_ref, w_ref, g_ref, nw_ref, sc_ref, sh_ref, x1_ref, h2_ref):
    x1, h2 = _mix_residual_norm(sb_ref, y_ref, x_ref, sbw_ref, w_ref, g_ref, nw_ref, sc_ref, sh_ref)
    x1_ref[...] = x1
    h2_ref[...] = h2.astype(h2_ref.dtype)


R_E1, R_E2, R_W1, R_W2, R_RANK1, R_RANK2 = range(6)


def _outproj_router_kernel(sb_ref, y_ref, x_ref, sbw_ref, w_ref, g_ref, nw_ref, sc_ref, sh_ref,
                           wrh_ref, wrl_ref, br_ref, tri_ref,
                           x1_ref, h2_ref, gates_ref, route_ref, count_ref):
    x1, h2 = _mix_residual_norm(sb_ref, y_ref, x_ref, sbw_ref, w_ref, g_ref, nw_ref, sc_ref, sh_ref)
    x1_ref[...] = x1
    h2_ref[...] = h2.astype(h2_ref.dtype)

    @pl.when(pl.program_id(0) == 0)
    def _():
        count_ref[...] = jnp.zeros_like(count_ref)

    h_hi = h2.astype(BF16)
    h_lo = (h2 - h_hi.astype(F32)).astype(BF16)
    dot = lambda a, b: jnp.dot(a, b, preferred_element_type=F32)
    logits = dot(h_hi, wrh_ref[...]) + (dot(h_lo, wrh_ref[...]) + dot(h_hi, wrl_ref[...])) + br_ref[...]
    lane = lax.broadcasted_iota(jnp.int32, logits.shape, 1)
    m1 = jnp.max(logits, axis=-1, keepdims=True)
    i1 = jnp.min(jnp.where(logits == m1, lane, LANES), axis=-1, keepdims=True)
    rest = jnp.where(lane == i1, -jnp.inf, logits)
    m2 = jnp.max(rest, axis=-1, keepdims=True)
    i2 = jnp.min(jnp.where(rest == m2, lane, LANES), axis=-1, keepdims=True)
    w1 = 1.0 / (1.0 + jnp.exp(m2 - m1))
    w2 = 1.0 - w1
    hot1 = lane == i1
    hot2 = lane == i2
    gates_ref[...] = jnp.where(hot1, w1, 0.0) + jnp.where(hot2, w2, 0.0)
    hot = jnp.where(hot1 | hot2, 1.0, 0.0)
    before = dot(tri_ref[...], hot.astype(BF16)) + count_ref[...]
    rank1 = jnp.sum(jnp.where(hot1, before, 0.0), axis=-1, keepdims=True)
    rank2 = jnp.sum(jnp.where(hot2, before, 0.0), axis=-1, keepdims=True)
    count_ref[...] += jnp.sum(hot, axis=0, keepdims=True)
    rec = jnp.zeros(logits.shape, F32)
    for ln, val in ((R_E1, i1.astype(F32)), (R_E2, i2.astype(F32)), (R_W1, w1), (R_W2, w2),
                    (R_RANK1, rank1), (R_RANK2, rank2)):
        rec = jnp.where(lane == ln, val, rec)
    route_ref[...] = rec


def _out_proj(sb, y, x, sbw, w_out, g, nw, sc, sh, tm, rows_per_stream, h2_dtype, router=None):
    t = x.shape[0]
    mr = g.shape[1]
    row = lambda n: pl.BlockSpec((tm, n), lambda i: (i, 0))
    full = lambda a: pl.BlockSpec(a.shape, lambda i: (0,) * a.ndim)
    ms = _mod_spec(tm, mr, rows_per_stream)
    in_specs = [row(SB_WIDTH), row(SSD_INNER), row(D_MODEL), full(sbw), full(w_out), ms, full(nw), ms, ms]
    out_specs = [row(D_MODEL), row(D_MODEL)]
    out_shape = [jax.ShapeDtypeStruct((t, D_MODEL), F32), jax.ShapeDtypeStruct((t, D_MODEL), h2_dtype)]
    args = [sb, y, x, sbw, w_out, g, nw, sc, sh]
    body = _outproj_kernel
    if router is not None:
        body = _outproj_router_kernel
        args += list(router)
        in_specs += [full(a) for a in router]
        out_specs += [row(LANES), row(LANES), pl.BlockSpec((1, LANES), lambda i: (0, 0))]
        out_shape += [jax.ShapeDtypeStruct((t, LANES), F32), jax.ShapeDtypeStruct((t, LANES), F32),
                      jax.ShapeDtypeStruct((1, LANES), F32)]
    return pl.pallas_call(
        body,
        grid=(t // tm,),
        in_specs=in_specs,
        out_specs=out_specs,
        out_shape=out_shape,
        compiler_params=_cparams(("arbitrary",)),
        name="out_proj" if router is None else "out_proj_router",
    )(*args)


def _ffn_kernel(h_ref, wg_ref, wu_ref, wd_ref, x_ref, g_ref, *rest, n_exp):
    gates_ref = rest[0] if n_exp > 1 else None
    o_ref, acc_ref = rest[-2:]
    e = pl.program_id(1)
    j = pl.program_id(2)

    @pl.when((e == 0) & (j == 0))
    def _():
        acc_ref[...] = jnp.zeros_like(acc_ref)

    part = _swiglu_partial(h_ref[...], wg_ref, wu_ref, wd_ref)
    if n_exp > 1:
        lane = lax.broadcasted_iota(jnp.int32, gates_ref.shape, 1)
        part = part * jnp.sum(jnp.where(lane == e, gates_ref[...], 0.0), axis=-1, keepdims=True)
    acc_ref[...] += part

    @pl.when((e == n_exp - 1) & (j == pl.num_programs(2) - 1))
    def _():
        o_ref[...] = x_ref[...] + g_ref[0] * acc_ref[...]


def _ffn(h, wg, wu, wd, x, g, gates, tm, tf, rows_per_stream):
    t = x.shape[0]
    n_exp = wg.shape[0]
    mr = g.shape[1]
    if mr == 1:
        gspec = pl.BlockSpec((1, 1, D_MODEL), lambda i, e, j: ((i * tm) // rows_per_stream, 0, 0))
    else:
        gspec = pl.BlockSpec((1, tm, D_MODEL), lambda i, e, j: (i, 0, 0))
    row = lambda n: pl.BlockSpec((tm, n), lambda i, e, j: (i, 0))
    in_specs = [row(D_MODEL),
                pl.BlockSpec((1, D_MODEL, tf), lambda i, e, j: (e, 0, j)),
                pl.BlockSpec((1, D_MODEL, tf), lambda i, e, j: (e, 0, j)),
                pl.BlockSpec((1, tf, D_MODEL), lambda i, e, j: (e, j, 0)),
                row(D_MODEL), gspec]
    args = [h, wg, wu, wd, x, g]
    if n_exp > 1:
        in_specs.append(row(LANES))
        args.append(gates)
    return pl.pallas_call(
        functools.partial(_ffn_kernel, n_exp=n_exp),
        grid=(t // tm, n_exp, D_FF // tf),
        in_specs=in_specs,
        out_specs=row(D_MODEL),
        out_shape=jax.ShapeDtypeStruct((t, D_MODEL), F32),
        scratch_shapes=[pltpu.VMEM((tm, D_MODEL), F32)],
        compiler_params=_cparams(("arbitrary", "arbitrary", "arbitrary")),
        name="ffn_dense" if n_exp == 1 else "ffn_experts",
    )(*args)


EXPERT_TILE = 512


def _dispatch_kernel(pos_ref, h_ref, zeros_hbm, xs_hbm, sem, *, tm):
    del zeros_hbm

    def copy(r, slot):
        return pltpu.make_async_copy(h_ref.at[pl.ds(r, 1)], xs_hbm.at[pl.ds(pos_ref[0, 0, slot * tm + r], 1)], sem)

    def issue(r, carry):
        copy(r, 0).start()
        copy(r, 1).start()
        return carry

    def drain(r, carry):
        copy(r, 0).wait()
        copy(r, 1).wait()
        return carry

    lax.fori_loop(0, tm, issue, 0, unroll=8)
    lax.fori_loop(0, tm, drain, 0, unroll=8)


def _dispatch(h, pos, n_rows, tm):
    t = h.shape[0]
    return pl.pallas_call(
        functools.partial(_dispatch_kernel, tm=tm),
        grid=(t // tm,),
        in_specs=[pl.BlockSpec((1, 1, 2 * tm), lambda i: (i, 0, 0), memory_space=pltpu.SMEM),
                  pl.BlockSpec((tm, D_MODEL), lambda i: (i, 0)), pl.BlockSpec(memory_space=pl.ANY)],
        out_specs=pl.BlockSpec(memory_space=pl.ANY),
        out_shape=jax.ShapeDtypeStruct((n_rows, D_MODEL), F32),
        scratch_shapes=[pltpu.SemaphoreType.DMA(())],
        input_output_aliases={2: 0},
        compiler_params=_cparams(("arbitrary",)),
        name="moe_dispatch",
    )(pos, h, jnp.zeros((n_rows, D_MODEL), F32))


def _swiglu_partial(h, wg_ref, wu_ref, wd_ref):
    gt = jnp.dot(h, wg_ref[0], preferred_element_type=F32)
    up = jnp.dot(h, wu_ref[0], preferred_element_type=F32)
    act = (_silu(gt) * up).astype(BF16)
    return jnp.dot(act, wd_ref[0], preferred_element_type=F32)


def _grouped_ffn_kernel(tile_expert_ref, n_active_ref, x_ref, wg_ref, wu_ref, wd_ref, o_ref):
    del tile_expert_ref
    j = pl.program_id(1)
    active = pl.program_id(0) < n_active_ref[0]

    @pl.when(active & (j == 0))
    def _():
        o_ref[...] = _swiglu_partial(x_ref[...].astype(BF16), wg_ref, wu_ref, wd_ref)

    @pl.when(active & (j > 0))
    def _():
        o_ref[...] += _swiglu_partial(x_ref[...].astype(BF16), wg_ref, wu_ref, wd_ref)

    @pl.when(jnp.logical_not(active) & (j == 0))
    def _():
        o_ref[...] = jnp.zeros_like(o_ref)


def _grouped_ffn(xs, tile_expert, n_active, wg, wu, wd, tf):
    n_rows = xs.shape[0]
    row = pl.BlockSpec((EXPERT_TILE, D_MODEL), lambda i, j, te, na: (i, 0))
    grid_spec = pltpu.PrefetchScalarGridSpec(
        num_scalar_prefetch=2,
        grid=(n_rows // EXPERT_TILE, D_FF // tf),
        in_specs=[row,
                  pl.BlockSpec((1, D_MODEL, tf), lambda i, j, te, na: (te[i], 0, j)),
                  pl.BlockSpec((1, D_MODEL, tf), lambda i, j, te, na: (te[i], 0, j)),
                  pl.BlockSpec((1, tf, D_MODEL), lambda i, j, te, na: (te[i], j, 0))],
        out_specs=row)
    return pl.pallas_call(
        _grouped_ffn_kernel,
        grid_spec=grid_spec,
        out_shape=jax.ShapeDtypeStruct((n_rows, D_MODEL), F32),
        compiler_params=_cparams(("arbitrary", "arbitrary")),
        name="ffn_grouped",
    )(tile_expert, n_active, xs, wg, wu, wd)


def _combine_kernel(pos_ref, nxt_ref, route_ref, x_ref, g_ref, ye_hbm, o_ref, buf_ref, sem_ref, *, tm):
    i = pl.program_id(0)
    n = pl.num_programs(0)
    slot = i % 2

    def copy(idx_ref, r, which, s):
        return pltpu.make_async_copy(ye_hbm.at[pl.ds(idx_ref[0, 0, which * tm + r], 1)],
                                     buf_ref.at[s, pl.ds(which * tm + r, 1)], sem_ref.at[s])

    def fetch(idx_ref, s):
        def issue(r, carry):
            copy(idx_ref, r, 0, s).start()
            copy(idx_ref, r, 1, s).start()
            return carry
        lax.fori_loop(0, tm, issue, 0, unroll=8)

    @pl.when(i == 0)
    def _():
        fetch(pos_ref, 0)

    @pl.when(i + 1 < n)
    def _():
        fetch(nxt_ref, 1 - slot)

    def drain(r, carry):
        copy(pos_ref, r, 0, slot).wait()
        copy(pos_ref, r, 1, slot).wait()
        return carry

    lax.fori_loop(0, tm, drain, 0, unroll=8)
    rec = route_ref[...]
    w1 = rec[:, R_W1:R_W1 + 1]
    w2 = rec[:, R_W2:R_W2 + 1]
    rows = buf_ref[slot]
    o_ref[...] = x_ref[...] + g_ref[0] * (w1 * rows[0:tm] + w2 * rows[tm:2 * tm])


def _combine(ye, pos, route, x, g, tm, rows_per_stream):
    t = x.shape[0]
    n = t // tm
    row = lambda w: pl.BlockSpec((tm, w), lambda i: (i, 0))
    smem = lambda f: pl.BlockSpec((1, 1, 2 * tm), f, memory_space=pltpu.SMEM)
    return pl.pallas_call(
        functools.partial(_combine_kernel, tm=tm),
        grid=(n,),
        in_specs=[smem(lambda i: (i, 0, 0)), smem(lambda i: (jnp.minimum(i + 1, n - 1), 0, 0)),
                  row(LANES), row(D_MODEL), _mod_spec(tm, g.shape[1], rows_per_stream),
                  pl.BlockSpec(memory_space=pl.ANY)],
        out_specs=row(D_MODEL),
        out_shape=jax.ShapeDtypeStruct((t, D_MODEL), F32),
        scratch_shapes=[pltpu.VMEM((2, 2 * tm, D_MODEL), F32), pltpu.SemaphoreType.DMA((2,))],
        compiler_params=_cparams(("arbitrary",)),
        name="moe_combine",
    )(pos, pos, route, x, g, ye)


def _sorted_experts(h2f, route, counts, x1, g_f, wg, wu, wd, tm, rows_per_stream):
    t = h2f.shape[0]
    n_tiles = (2 * t) // EXPERT_TILE + N_EXPERTS
    cnt = counts[0, :N_EXPERTS].astype(jnp.int32)
    padded = ((cnt + EXPERT_TILE - 1) // EXPERT_TILE) * EXPERT_TILE
    ends = jnp.cumsum(padded)
    starts = ends - padded
    experts = jnp.arange(N_EXPERTS, dtype=jnp.int32)

    def slots(e_lane, r_lane):
        e = route[:, e_lane].astype(jnp.int32)
        start = jnp.sum(jnp.where(e[:, None] == experts[None, :], starts[None, :], 0), axis=-1)
        return start + route[:, r_lane].astype(jnp.int32)

    pos = jnp.stack([slots(R_E1, R_RANK1).reshape(t // tm, tm), slots(R_E2, R_RANK2).reshape(t // tm, tm)], axis=1)
    pos = pos.reshape(t // tm, 1, 2 * tm)
    tile_start = jnp.arange(n_tiles, dtype=jnp.int32) * EXPERT_TILE
    n_active = (ends[-1] // EXPERT_TILE).astype(jnp.int32)
    clipped = jnp.minimum(tile_start, ends[-1] - EXPERT_TILE)
    tile_expert = jnp.sum((clipped[:, None] >= ends[None, :]).astype(jnp.int32), axis=-1)
    xs = _dispatch(h2f, pos, n_tiles * EXPERT_TILE, tm)
    ye = _grouped_ffn(xs, tile_expert, n_active.reshape(1), wg, wu, wd, D_FF // 2)
    return _combine(ye, pos, route, x1, g_f, tm, rows_per_stream)


def _layer(x, mod, l, p, consts, kv_states, k_past, v_past, s0, c0, tm, ssd_chunk, attn_blk, expand_mod,
           sorted_experts):
    b, s, _ = x.shape
    t = b * s

    def mod_rows(i):
        m = mod[:, i, :]
        if expand_mod:
            return jnp.repeat(m, s, axis=0).reshape(t // tm, tm, D_MODEL)
        return m.reshape(b, 1, D_MODEL)

    sh_m, sc_m, g_m, sh_f, sc_f, g_f = [mod_rows(i) for i in range(6)]
    xf = x.reshape(t, D_MODEL)
    q, k, v, kf, vf, z, xbc, dt = _in_proj(xf, sc_m, sh_m, p['norm_mix_w'][l], p['w_in'][l], consts['head_ones'],
                                           p['q_norm_w'][l], p['k_norm_w'][l], tm, s, l, p['w_in'].shape[0], kv_states)
    shp = lambda a: a.reshape(b, s, a.shape[-1])
    if k_past is None:
        sb = _attn_prompt(shp(q), shp(k), shp(v), consts['from_ones'], attn_blk)
    else:
        past = k_past.shape[0] // (p['w_in'].shape[0] * b * N_HEADS)
        sb = _attn_sample(shp(q), shp(k), shp(v), k_past, v_past, past, l, consts['from_ones'], attn_blk)
    y, s_new, c_new = _ssd(shp(xbc), shp(z), shp(dt), p['conv_w'][l], p['conv_b'][l], p['dt_bias'][l],
                           p['a_log'][l], p['d_skip'][l], p['ssd_norm_w'][l], consts['head_expand'],
                           s0, c0, ssd_chunk)
    mix_args = (sb.reshape(t, SB_WIDTH), y.reshape(t, SSD_INNER), xf, p['sb_norm_w'][l], p['w_out'][l], g_m,
                p['norm_ffn_w'][l], sc_f, sh_f, tm, s)
    i = l // 2
    tf = D_FF // 2
    if l % 2 == 0:
        x1, h2 = _out_proj(*mix_args, BF16)
        x2 = _ffn(h2, p['w_gate_dense'][i:i + 1], p['w_up_dense'][i:i + 1], p['w_down_dense'][i:i + 1],
                  x1, g_f, None, tm, tf, s)
    else:
        router = (p['w_router_hi'][i], p['w_router_lo'][i], p['b_router'][i], consts['before_ones'][:tm, :tm])
        experts = (p['w_gate_moe'][i], p['w_up_moe'][i], p['w_down_moe'][i])
        if sorted_experts:
            x1, h2, _, route, counts = _out_proj(*mix_args, F32, router)
            x2 = _sorted_experts(h2, route, counts, x1, g_f, *experts, tm // 2, s)
        else:
            x1, h2, gates, _, _ = _out_proj(*mix_args, BF16, router)
            x2 = _ffn(h2, *experts, x1, g_f, gates, tm, tf, s)
    return x2.reshape(b, s, D_MODEL), kf, vf, s_new, c_new


def kernel(x_prompt, x_sample, c_prompt, c_sample, cache_sb_k, cache_sb_v, state_ssd, state_conv, w_mod, b_mod, norm_mix_w, norm_ffn_w, w_in, q_norm_w, k_norm_w, sb_norm_w, conv_w, conv_b, dt_bias, a_log, d_skip, ssd_norm_w, w_out, w_gate_dense, w_up_dense, w_down_dense, w_router, b_router, w_gate_moe, w_up_moe, w_down_moe):
    depth = w_in.shape[0]
    bp, sp, _ = x_prompt.shape
    bs, ss, _ = x_sample.shape
    past = cache_sb_k.shape[2]
    n_moe = w_router.shape[0]

    row = lambda a: a.reshape(depth, 1, a.shape[-1])
    lane_pad = lambda a: jnp.pad(a, ((0, 0), (0, LANES - a.shape[-1]))).reshape(a.shape[0], 1, LANES)
    w_router_pad = jnp.pad(w_router, ((0, 0), (0, 0), (0, LANES - N_EXPERTS)))
    p = {
        'norm_mix_w': row(norm_mix_w), 'norm_ffn_w': row(norm_ffn_w),
        'w_in': jnp.pad(w_in, ((0, 0), (0, 0), (0, IN_MAIN + DT_PAD - w_in.shape[-1]))).astype(BF16),
        'q_norm_w': row(jnp.tile(q_norm_w, (1, N_HEADS))), 'k_norm_w': row(jnp.tile(k_norm_w, (1, N_HEADS))),
        'sb_norm_w': row(sb_norm_w), 'conv_w': conv_w, 'conv_b': row(conv_b),
        'dt_bias': lane_pad(dt_bias), 'a_log': lane_pad(a_log),
        'd_skip': row(jnp.repeat(d_skip, HEAD_DIM, axis=-1)), 'ssd_norm_w': row(ssd_norm_w),
        'w_out': w_out.astype(BF16),
        'w_gate_dense': w_gate_dense.astype(BF16), 'w_up_dense': w_up_dense.astype(BF16),
        'w_down_dense': w_down_dense.astype(BF16),
        'w_router_hi': w_router_pad.astype(BF16),
        'w_router_lo': (w_router_pad - w_router_pad.astype(BF16).astype(F32)).astype(BF16),
        'b_router': jnp.pad(b_router, ((0, 0), (0, LANES - N_EXPERTS)), constant_values=-1e30).reshape(n_moe, 1, LANES),
        'w_gate_moe': w_gate_moe.astype(BF16), 'w_up_moe': w_up_moe.astype(BF16),
        'w_down_moe': w_down_moe.astype(BF16),
    }
    attn_blk = 256
    tm_prompt = 512
    idx = jnp.arange(attn_blk, dtype=jnp.int32)
    wide = jnp.arange(SB_WIDTH, dtype=jnp.int32)
    tok = jnp.arange(tm_prompt, dtype=jnp.int32)
    consts = {
        'from_ones': (idx[:, None] >= idx[None, :]).astype(BF16),
        'before_ones': (tok[None, :] < tok[:, None]).astype(BF16),
        'head_ones': (wide[:, None] // HEAD_DIM == wide[None, :] // HEAD_DIM).astype(BF16),
        'head_expand': (jnp.arange(LANES, dtype=jnp.int32)[:, None] == wide[None, :] // HEAD_DIM).astype(BF16),
    }

    c_all = jnp.concatenate([c_prompt, c_sample], axis=0)
    mod = _modulation(c_all, w_mod, b_mod).reshape(depth, bp + bs, 6, D_MODEL)

    kp = cache_sb_k.reshape(depth * bs * past * N_HEADS, HEAD_DIM)
    vp = cache_sb_v.reshape(depth * bs * past * N_HEADS, HEAD_DIM)
    halo_pad = ((0, 0), (0, 0), (CONV_HALO - (CONV_WIDTH - 1), 0), (0, 0))
    conv_in = jnp.pad(state_conv, halo_pad)
    ssd_in = state_ssd.reshape(depth, bs, N_HEADS * HEAD_DIM, D_STATE)
    zero_s = jnp.zeros((bp, N_HEADS * HEAD_DIM, D_STATE), F32)
    zero_c = jnp.zeros((bp, CONV_HALO, CONV_DIM), F32)

    xp, xs = x_prompt, x_sample
    kv_p = kv_s = None
    outs_p, outs_s = [], []
    for l in range(depth):
        xp, k, v, s, c = _layer(xp, mod[l, :bp], l, p, consts, kv_p, None, None, zero_s, zero_c,
                                tm=tm_prompt, ssd_chunk=256, attn_blk=attn_blk, expand_mod=False, sorted_experts=True)
        kv_p = (k, v)
        outs_p.append((s.reshape(bp, N_HEADS, HEAD_DIM, D_STATE), c[:, CONV_HALO - (CONV_WIDTH - 1):]))
        xs, k, v, s, c = _layer(xs, mod[l, bp:], l, p, consts, kv_s, kp, vp, ssd_in[l], conv_in[l],
                                tm=bs * ss, ssd_chunk=ss, attn_blk=attn_blk, expand_mod=True, sorted_experts=False)
        kv_s = (k, v)
        outs_s.append((s.reshape(bs, N_HEADS, HEAD_DIM, D_STATE), c[:, CONV_HALO - (CONV_WIDTH - 1):]))
    stack = lambda outs, i: jnp.stack([o[i] for o in outs])
    heads = lambda a, b, s: a.reshape(depth, b, s, N_HEADS, HEAD_DIM)
    return (xp, xs,
            heads(kv_p[0], bp, sp), heads(kv_p[1], bp, sp), stack(outs_p, 0), stack(outs_p, 1),
            heads(kv_s[0], bs, ss), heads(kv_s[1], bs, ss), stack(outs_s, 0), stack(outs_s, 1))
```

```python
import functools

import jax
import jax.numpy as jnp
from jax import lax
from jax.experimental import pallas as pl
from jax.experimental.pallas import tpu as pltpu

F32 = jnp.float32
BF16 = jnp.bfloat16
HIGHEST = lax.Precision.HIGHEST

EPS = 1e-6
LOG2_E = 1.4426950408889634
D_MODEL = 1024
N_HEADS = 8
HEAD_DIM = 64
SB_WIDTH = 512
SSD_INNER = 512
SSD_GROUPS = 2
D_STATE = 128
CONV_WIDTH = 4
CONV_DIM = 1024
IN_MAIN = 3072
DT_PAD = 128
D_FF = 2816
N_EXPERTS = 8
LANES = 128
CONV_HALO = 8

VMEM_LIMIT = 56 * 1024 * 1024


def _cparams(sem, vmem=VMEM_LIMIT):
    return pltpu.CompilerParams(dimension_semantics=sem, vmem_limit_bytes=vmem)


def _silu(x):
    return x * (1.0 / (1.0 + jnp.exp2(x * -LOG2_E)))


def _softplus(x):
    return jnp.maximum(x, 0.0) + jnp.log(1.0 + jnp.exp(-jnp.abs(x)))


def _rms(x):
    return x * lax.rsqrt(jnp.mean(x * x, axis=-1, keepdims=True) + EPS)


def _mod_kernel(c_ref, w_ref, b_ref, o_ref):
    s = _silu(c_ref[...])
    o_ref[0] = jnp.dot(s, w_ref[0], preferred_element_type=F32, precision=HIGHEST) + b_ref[0]


def _modulation(c_all, w_mod, b_mod):
    depth, _, n6 = w_mod.shape
    ns = c_all.shape[0]
    nj = n6 // D_MODEL
    return pl.pallas_call(
        _mod_kernel,
        grid=(depth, nj),
        in_specs=[pl.BlockSpec((ns, D_MODEL), lambda l, j: (0, 0)),
                  pl.BlockSpec((1, D_MODEL, D_MODEL), lambda l, j: (l, 0, j)),
                  pl.BlockSpec((1, 1, D_MODEL), lambda l, j: (l, 0, j))],
        out_specs=pl.BlockSpec((1, ns, D_MODEL), lambda l, j: (l, 0, j)),
        out_shape=jax.ShapeDtypeStruct((depth, ns, n6), F32),
        compiler_params=_cparams(("arbitrary", "arbitrary")),
        name="adaln_mod",
    )(c_all, w_mod, b_mod.reshape(depth, 1, n6))


def _inproj_kernel(x_ref, sc_ref, sh_ref, nw_ref, w_ref, bd_ref, qw_ref, kw_ref, *rest, nt):
    kf_ref, vf_ref = rest[-5:-3]

    @pl.when(pl.program_id(0) < nt)
    def _():
        _inproj_tile(x_ref, sc_ref, sh_ref, nw_ref, w_ref, bd_ref, qw_ref, kw_ref, *rest[-8:])

    @pl.when(pl.program_id(0) >= nt)
    def _():
        kf_ref[...] = jnp.zeros_like(kf_ref)
        vf_ref[...] = jnp.zeros_like(vf_ref)


def _inproj_tile(x_ref, sc_ref, sh_ref, nw_ref, w_ref, bd_ref, qw_ref, kw_ref,
                 q_ref, k_ref, v_ref, kf_ref, vf_ref, z_ref, xbc_ref, dt_ref):
    h = _rms(x_ref[...]) * nw_ref[...]
    h = h * (1.0 + sc_ref[0]) + sh_ref[0]
    proj = jnp.dot(h.astype(BF16), w_ref[...], preferred_element_type=F32)

    def head_norm(t, w):
        ss = jnp.dot((t * t).astype(BF16), bd_ref[...], preferred_element_type=F32)
        return t * lax.rsqrt(ss * (1.0 / HEAD_DIM) + EPS) * w

    q = head_norm(proj[:, 0:SB_WIDTH], qw_ref[...])
    k = head_norm(proj[:, SB_WIDTH:2 * SB_WIDTH], kw_ref[...])
    v = proj[:, 2 * SB_WIDTH:3 * SB_WIDTH]
    q_ref[...] = (q * (HEAD_DIM ** -0.5 * LOG2_E)).astype(BF16)
    k_ref[...] = k.astype(BF16)
    v_ref[...] = v.astype(BF16)
    for hd in range(N_HEADS):
        cols = slice(hd * HEAD_DIM, (hd + 1) * HEAD_DIM)
        kf_ref[pl.ds(hd, k.shape[0], stride=N_HEADS), :] = k[:, cols]
        vf_ref[pl.ds(hd, v.shape[0], stride=N_HEADS), :] = v[:, cols]
    z_ref[...] = proj[:, 3 * SB_WIDTH:3 * SB_WIDTH + SSD_INNER].astype(BF16)
    xbc_ref[...] = proj[:, 3 * SB_WIDTH + SSD_INNER:IN_MAIN]
    dt_ref[...] = proj[:, IN_MAIN:IN_MAIN + DT_PAD]


def _mod_spec(tm, mr, rows_per_stream, tile=lambda i: i):
    if mr == 1:
        return pl.BlockSpec((1, 1, D_MODEL), lambda i: ((tile(i) * tm) // rows_per_stream, 0, 0))
    return pl.BlockSpec((1, tm, D_MODEL), lambda i: (tile(i), 0, 0))


def _in_proj(x, sc, sh, nw, w_pad, bd, qw, kw, tm, rows_per_stream, layer, depth, kv_states):
    t = x.shape[0]
    mr = sc.shape[1]
    nt = t // tm
    first = kv_states is None
    assert first == (layer == 0)
    tile = lambda i: jnp.minimum(i, nt - 1)
    row = lambda n: pl.BlockSpec((tm, n), lambda i: (tile(i), 0))
    full = lambda a: pl.BlockSpec(a.shape, lambda i: (0,) * a.ndim)
    state = pl.BlockSpec((tm * N_HEADS, HEAD_DIM), lambda i: (layer * nt + i, 0))
    state_shape = jax.ShapeDtypeStruct((depth * t * N_HEADS, HEAD_DIM), F32)
    outs = [(SB_WIDTH, BF16), (SB_WIDTH, BF16), (SB_WIDTH, BF16), None, None,
            (SSD_INNER, BF16), (CONV_DIM, F32), (DT_PAD, F32)]
    in_specs = [row(D_MODEL), _mod_spec(tm, mr, rows_per_stream, tile), _mod_spec(tm, mr, rows_per_stream, tile),
                full(nw), full(w_pad), full(bd), full(qw), full(kw)]
    args = [x, sc, sh, nw, w_pad, bd, qw, kw]
    aliases = {}
    if not first:
        in_specs += [pl.BlockSpec(memory_space=pl.ANY)] * 2
        aliases = {len(args): 3, len(args) + 1: 4}
        args += list(kv_states)
    return pl.pallas_call(
        functools.partial(_inproj_kernel, nt=nt),
        grid=(depth * nt if first else nt,),
        in_specs=in_specs,
        out_specs=[state if o is None else row(o[0]) for o in outs],
        out_shape=[state_shape if o is None else jax.ShapeDtypeStruct((t, o[0]), o[1]) for o in outs],
        input_output_aliases=aliases,
        compiler_params=_cparams(("arbitrary",)),
        name="in_proj",
    )(*args)


def _dot_nt(a, b):
    return lax.dot_general(a, b, (((1,), (1,)), ((), ())), preferred_element_type=F32)


def _sb_accumulate(blocks, from_ones, acc_ref, c_ref):
    acc = acc_ref[...]
    c = c_ref[...]
    for z, weighted, mask in blocks:
        u = jnp.maximum(z, 0.0) + jnp.log(1.0 + jnp.exp2(-jnp.abs(z))) * LOG2_E
        if mask is not None:
            u = jnp.where(mask, u, 0.0)
        cum = jnp.dot(u.astype(BF16), from_ones, preferred_element_type=F32)
        a = jnp.exp2(z - cum)
        if mask is not None:
            a = jnp.where(mask, a, 0.0)
        acc = acc + jnp.exp2(-c) * weighted(a.astype(BF16))
        c = c + cum[:, 0:1]
    acc_ref[...] = acc
    c_ref[...] = c


def _sb_blocks(qs, blocks, from_ones, acc_ref, c_ref):
    _sb_accumulate([(_dot_nt(qs, kblk), functools.partial(jnp.dot, b=vblk, preferred_element_type=F32), mask)
                    for kblk, vblk, mask in blocks], from_ones, acc_ref, c_ref)


def _stack_heads(q):
    lane = lax.broadcasted_iota(jnp.int32, q.shape, 1)
    zero = jnp.zeros_like(q)
    return jnp.concatenate([jnp.where(lane < HEAD_DIM, q, zero), jnp.where(lane >= HEAD_DIM, q, zero)], axis=0)


def _unstack_heads(acc, tq):
    lane = lax.broadcasted_iota(jnp.int32, (tq, LANES), 1)
    return jnp.where(lane < HEAD_DIM, acc[0:tq], acc[tq:2 * tq])


def _strict_lower(n, m):
    r = lax.broadcasted_iota(jnp.int32, (n, m), 0)
    c = lax.broadcasted_iota(jnp.int32, (n, m), 1)
    return c < r


def _stacked_causal(tq, tk):
    r = lax.broadcasted_iota(jnp.int32, (2 * tq, tk), 0)
    c = lax.broadcasted_iota(jnp.int32, (2 * tq, tk), 1)
    return c < jnp.where(r >= tq, r - tq, r)


def _attn_prompt_kernel(q_ref, k_ref, v_ref, m_ref, o_ref, acc_ref, c_ref, *, blk):
    i = pl.program_id(2)
    qs = _stack_heads(q_ref[0])
    acc_ref[...] = jnp.zeros_like(acc_ref)
    c_ref[...] = jnp.zeros_like(c_ref)
    from_ones = m_ref[...]
    diag_mask = _stacked_causal(blk, blk)

    def kv(kb, mask=None):
        start = pl.multiple_of(kb * blk, blk)
        return k_ref[0, pl.ds(start, blk), :], v_ref[0, pl.ds(start, blk), :], mask

    def run(top, n):
        _sb_blocks(qs, [kv(top - d) for d in range(n)], from_ones, acc_ref, c_ref)

    @pl.when(i == 0)
    def _():
        _sb_blocks(qs, [kv(i, diag_mask)], from_ones, acc_ref, c_ref)

    @pl.when(i > 0)
    def _():
        _sb_blocks(qs, [kv(i, diag_mask), kv(i - 1)], from_ones, acc_ref, c_ref)

    rest = jnp.maximum(i - 1, 0)
    n4 = rest // 4

    def body(j, carry):
        run(rest - 1 - 4 * j, 4)
        return carry

    lax.fori_loop(0, n4, body, 0)
    tail = rest - 4 * n4

    @pl.when(tail >= 2)
    def _():
        run(tail - 1, 2)

    @pl.when(tail % 2 == 1)
    def _():
        run(0, 1)

    o_ref[0] = _unstack_heads(acc_ref[...], blk).astype(o_ref.dtype)


def _attn_prompt(q, k, v, from_ones, blk):
    b, s, _ = q.shape
    npair = SB_WIDTH // LANES
    qspec = pl.BlockSpec((1, blk, LANES), lambda bi, p, i: (bi, i, p))
    kvspec = pl.BlockSpec((1, s, LANES), lambda bi, p, i: (bi, 0, p))
    return pl.pallas_call(
        functools.partial(_attn_prompt_kernel, blk=blk),
        grid=(b, npair, s // blk),
        in_specs=[qspec, kvspec, kvspec, pl.BlockSpec(from_ones.shape, lambda bi, p, i: (0, 0))],
        out_specs=qspec,
        out_shape=jax.ShapeDtypeStruct((b, s, SB_WIDTH), BF16),
        scratch_shapes=[pltpu.VMEM((2 * blk, LANES), F32), pltpu.VMEM((2 * blk, 1), F32)],
        compiler_params=_cparams(("arbitrary", "arbitrary", "arbitrary")),
        name="sb_attn_prompt",
    )(q, k, v, from_ones)


PAST_BLOCKS_PER_STEP = 2


def _attn_sample_kernel(q_ref, kn_ref, vn_ref, kp_ref, vp_ref, m_ref, o_ref, qbd_ref, acc_ref, c_ref, *, blk):
    j = pl.program_id(1)
    tq = q_ref.shape[1]
    rows = N_HEADS * tq
    from_ones = m_ref[...]
    row_head = lax.broadcasted_iota(jnp.int32, (rows, SB_WIDTH), 0) // tq
    lane_head = lax.broadcasted_iota(jnp.int32, (rows, SB_WIDTH), 1) // HEAD_DIM

    @pl.when(j == 0)
    def _():
        q = q_ref[0]
        stacked = jnp.concatenate([q] * N_HEADS, axis=0)
        qbd_ref[...] = jnp.where(row_head == lane_head, stacked, jnp.zeros_like(stacked))
        acc_ref[...] = jnp.zeros_like(acc_ref)
        c_ref[...] = jnp.zeros_like(c_ref)
        r = lax.broadcasted_iota(jnp.int32, (rows, blk), 0)
        mask = lax.broadcasted_iota(jnp.int32, (rows, blk), 1) < r - (r // tq) * tq
        block = (_dot_nt(qbd_ref[...], kn_ref[0]), functools.partial(jnp.dot, b=vn_ref[0], preferred_element_type=F32),
                 mask)
        _sb_accumulate([block], from_ones, acc_ref, c_ref)

    @pl.when(j > 0)
    def _():
        blocks = []
        for sub in reversed(range(PAST_BLOCKS_PER_STEP)):
            cols = slice(sub * blk, (sub + 1) * blk)
            z = jnp.dot(qbd_ref[...], kp_ref[:, cols].astype(BF16), preferred_element_type=F32)
            blocks.append((z, functools.partial(_dot_nt, b=vp_ref[:, cols].astype(BF16)), None))
        _sb_accumulate(blocks, from_ones, acc_ref, c_ref)

    @pl.when(j == pl.num_programs(1) - 1)
    def _():
        kept = jnp.where(row_head == lane_head, acc_ref[...], 0.0)
        out = kept[0:tq]
        for hd in range(1, N_HEADS):
            out = out + kept[hd * tq:(hd + 1) * tq]
        o_ref[0] = out.astype(o_ref.dtype)


def _attn_sample(q, k_new, v_new, k_past_t, v_past_t, layer, from_ones, blk):
    b, tq, _ = q.shape
    past = k_past_t.shape[1]
    span = PAST_BLOCKS_PER_STEP * blk
    nsteps = past // span
    assert nsteps * span == past
    nspec = pl.BlockSpec((1, tq, SB_WIDTH), lambda bi, j: (bi, 0, 0))
    kspec = pl.BlockSpec((1, blk, SB_WIDTH), lambda bi, j: (bi, 0, 0))
    pspec = pl.BlockSpec((SB_WIDTH, span), lambda bi, j: (layer * b + bi, nsteps - jnp.maximum(j, 1)))
    pad = ((0, 0), (0, blk - tq), (0, 0))
    k_new, v_new = jnp.pad(k_new, pad), jnp.pad(v_new, pad)
    rows = N_HEADS * tq
    return pl.pallas_call(
        functools.partial(_attn_sample_kernel, blk=blk),
        grid=(b, 1 + nsteps),
        in_specs=[nspec, kspec, kspec, pspec, pspec, pl.BlockSpec(from_ones.shape, lambda bi, j: (0, 0))],
        out_specs=nspec,
        out_shape=jax.ShapeDtypeStruct((b, tq, SB_WIDTH), BF16),
        scratch_shapes=[pltpu.VMEM((rows, SB_WIDTH), BF16), pltpu.VMEM((rows, SB_WIDTH), F32),
                        pltpu.VMEM((rows, 1), F32)],
        compiler_params=_cparams(("arbitrary", "arbitrary")),
        name="sb_attn_sample",
    )(q, k_new, v_new, k_past_t, v_past_t, from_ones)


def _ssd_kernel(xbc_ref, z_ref, dt_ref, cw_ref, cb_ref, dtb_ref, alog_ref, dsk_ref, nw_ref, exp_ref,
                s0_ref, c0_ref, y_ref, sT_out_ref, cs_out_ref, ext_ref, st_ref, *, L):
    ci = pl.program_id(1)
    nc = pl.num_programs(1)
    gw = SSD_INNER // SSD_GROUPS
    hpg = N_HEADS // SSD_GROUPS

    @pl.when(ci == 0)
    def _():
        ext_ref[0:CONV_HALO, :] = c0_ref[0]
        st_ref[...] = s0_ref[0].T

    ext_ref[CONV_HALO:CONV_HALO + L, :] = xbc_ref[0]
    conv = cb_ref[...]
    for w in range(CONV_WIDTH):
        off = CONV_HALO - (CONV_WIDTH - 1) + w
        conv = conv + ext_ref[off:off + L, :] * cw_ref[w:w + 1, :]
    xbc = _silu(conv)
    tail = ext_ref[L:L + CONV_HALO, :]
    ext_ref[0:CONV_HALO, :] = tail

    x = xbc[:, 0:SSD_INNER]
    bm = xbc[:, SSD_INNER:SSD_INNER + SSD_GROUPS * D_STATE]
    cm = xbc[:, SSD_INNER + SSD_GROUPS * D_STATE:]
    dt = _softplus(dt_ref[0] + dtb_ref[...])
    da = dt * (-jnp.exp(alog_ref[...]))
    tril = jnp.where(_strict_lower(L, L) | (lax.broadcasted_iota(jnp.int32, (L, L), 0)
                                            == lax.broadcasted_iota(jnp.int32, (L, L), 1)), 1.0, 0.0)
    a_cum = jnp.dot(tril, da * LOG2_E, preferred_element_type=F32, precision=HIGHEST)
    a_cum_t = a_cum.T
    a_last = a_cum[L - 1:L, :]
    expand = exp_ref[...]

    def bcast(t):
        hi = t.astype(BF16)
        lo = (t - hi.astype(F32)).astype(BF16)
        return (jnp.dot(hi, expand, preferred_element_type=F32) + jnp.dot(lo, expand, preferred_element_type=F32))

    per_head = bcast(jnp.concatenate([dt, jnp.exp2(a_cum), jnp.exp2(a_last - a_cum)], axis=0))
    dt_e, ea_e, ds_e = per_head[0:L], per_head[L:2 * L], per_head[2 * L:3 * L]
    xdt = x * dt_e
    wgt = (xdt * ds_e).astype(BF16)
    xdt_b = xdt.astype(BF16)
    bm_t = bm.T.astype(BF16)
    bm_b = bm.astype(BF16)
    cm_b = cm.astype(BF16)
    causal = _strict_lower(L, L) | (lax.broadcasted_iota(jnp.int32, (L, L), 0)
                                    == lax.broadcasted_iota(jnp.int32, (L, L), 1))
    lane_g = lax.broadcasted_iota(jnp.int32, (L, gw), 1)
    y_parts = []
    for g in range(SSD_GROUPS):
        gs = slice(g * gw, (g + 1) * gw)
        ns = slice(g * D_STATE, (g + 1) * D_STATE)
        cb = lax.dot_general(cm_b[:, ns], bm_b[:, ns], (((1,), (1,)), ((), ())), preferred_element_type=F32)
        st_g = st_ref[:, gs]
        y_g = jnp.dot(cm_b[:, ns], st_g.astype(BF16), preferred_element_type=F32) * ea_e[:, gs]
        for hl in range(hpg):
            hd = g * hpg + hl
            seg = a_cum[:, hd:hd + 1] - a_cum_t[hd:hd + 1, :]
            decay = jnp.where(causal, jnp.exp2(seg), 0.0)
            scores = (cb * decay).astype(BF16)
            xh = jnp.where((lane_g >= hl * HEAD_DIM) & (lane_g < (hl + 1) * HEAD_DIM), xdt_b[:, gs],
                           jnp.zeros_like(xdt_b[:, gs]))
            y_g = y_g + jnp.dot(scores, xh, preferred_element_type=F32)
        y_parts.append(y_g)
        new_states = jnp.dot(bm_t[ns, :], wgt[:, gs], preferred_element_type=F32)
        st_ref[:, gs] = st_g * ea_e[L - 1:L, gs] + new_states
    y = jnp.concatenate(y_parts, axis=-1) + dsk_ref[...] * x
    y = y * _silu(z_ref[0].astype(F32))
    y = jnp.concatenate([_rms(y[:, g * gw:(g + 1) * gw]) for g in range(SSD_GROUPS)], axis=-1) * nw_ref[...]
    y_ref[0] = y.astype(y_ref.dtype)

    @pl.when(ci == nc - 1)
    def _():
        sT_out_ref[0] = st_ref[...].T
        cs_out_ref[0] = tail


def _ssd(xbc, z, dt, cw, cb, dtb, alog, dsk_e, nw, expand, s0, c0, L):
    b, s, _ = xbc.shape
    seq = lambda n: pl.BlockSpec((1, L, n), lambda bi, ci: (bi, ci, 0))
    full = lambda a: pl.BlockSpec(a.shape, lambda bi, ci: (0,) * a.ndim)
    per_b = lambda a: pl.BlockSpec((1,) + a.shape[1:], lambda bi, ci: (bi,) + (0,) * (a.ndim - 1))
    return pl.pallas_call(
        functools.partial(_ssd_kernel, L=L),
        grid=(b, s // L),
        in_specs=[seq(CONV_DIM), seq(SSD_INNER), seq(DT_PAD), full(cw), full(cb), full(dtb), full(alog),
                  full(dsk_e), full(nw), full(expand), per_b(s0), per_b(c0)],
        out_specs=[seq(SSD_INNER), per_b(s0), per_b(c0)],
        out_shape=[jax.ShapeDtypeStruct((b, s, SSD_INNER), BF16),
                   jax.ShapeDtypeStruct(s0.shape, F32), jax.ShapeDtypeStruct(c0.shape, F32)],
        scratch_shapes=[pltpu.VMEM((L + CONV_HALO, CONV_DIM), F32), pltpu.VMEM((D_STATE, SSD_INNER), F32)],
        compiler_params=_cparams(("arbitrary", "arbitrary")),
        name="ssd_mixer",
    )(xbc, z, dt, cw, cb, dtb, alog, dsk_e, nw, expand, s0, c0)


def _mix_residual_norm(sb_ref, y_ref, x_ref, sbw_ref, w_ref, g_ref, nw_ref, sc_ref, sh_ref):
    sbn = _rms(sb_ref[...].astype(F32)) * sbw_ref[...]
    cat = jnp.concatenate([sbn.astype(BF16), y_ref[...]], axis=-1)
    mix = jnp.dot(cat, w_ref[...], preferred_element_type=F32)
    x1 = x_ref[...] + g_ref[0] * mix
    h2 = _rms(x1) * nw_ref[...]
    return x1, h2 * (1.0 + sc_ref[0]) + sh_ref[0]


def _outproj_kernel(sb_ref, y_ref, x_ref, sbw_ref, w_ref, g_ref, nw_ref, sc_ref, sh_ref, x1_ref, h2_ref):
    x1, h2 = _mix_residual_norm(sb_ref, y_ref, x_ref, sbw_ref, w_ref, g_ref, nw_ref, sc_ref, sh_ref)
    x1_ref[...] = x1
    h2_ref[...] = h2.astype(h2_ref.dtype)


R_E1, R_E2, R_W1, R_W2, R_RANK1, R_RANK2 = range(6)


def _outproj_router_kernel(sb_ref, y_ref, x_ref, sbw_ref, w_ref, g_ref, nw_ref, sc_ref, sh_ref,
                           wrh_ref, wrl_ref, br_ref, tri_ref,
                           x1_ref, h2_ref, gates_ref, route_ref, count_ref):
    x1, h2 = _mix_residual_norm(sb_ref, y_ref, x_ref, sbw_ref, w_ref, g_ref, nw_ref, sc_ref, sh_ref)
    x1_ref[...] = x1
    h2_ref[...] = h2.astype(h2_ref.dtype)

    @pl.when(pl.program_id(0) == 0)
    def _():
        count_ref[...] = jnp.zeros_like(count_ref)

    h_hi = h2.astype(BF16)
    h_lo = (h2 - h_hi.astype(F32)).astype(BF16)
    dot = lambda a, b: jnp.dot(a, b, preferred_element_type=F32)
    logits = dot(h_hi, wrh_ref[...]) + (dot(h_lo, wrh_ref[...]) + dot(h_hi, wrl_ref[...])) + br_ref[...]
    lane = lax.broadcasted_iota(jnp.int32, logits.shape, 1)
    m1 = jnp.max(logits, axis=-1, keepdims=True)
    i1 = jnp.min(jnp.where(logits == m1, lane, LANES), axis=-1, keepdims=True)
    rest = jnp.where(lane == i1, -jnp.inf, logits)
    m2 = jnp.max(rest, axis=-1, keepdims=True)
    i2 = jnp.min(jnp.where(rest == m2, lane, LANES), axis=-1, keepdims=True)
    w1 = 1.0 / (1.0 + jnp.exp(m2 - m1))
    w2 = 1.0 - w1
    hot1 = lane == i1
    hot2 = lane == i2
    gates_ref[...] = jnp.where(hot1, w1, 0.0) + jnp.where(hot2, w2, 0.0)
    hot = jnp.where(hot1 | hot2, 1.0, 0.0)
    before = dot(tri_ref[...], hot.astype(BF16)) + count_ref[...]
    rank1 = jnp.sum(jnp.where(hot1, before, 0.0), axis=-1, keepdims=True)
    rank2 = jnp.sum(jnp.where(hot2, before, 0.0), axis=-1, keepdims=True)
    count_ref[...] += jnp.sum(hot, axis=0, keepdims=True)
    rec = jnp.zeros(logits.shape, F32)
    for ln, val in ((R_E1, i1.astype(F32)), (R_E2, i2.astype(F32)), (R_W1, w1), (R_W2, w2),
                    (R_RANK1, rank1), (R_RANK2, rank2)):
        rec = jnp.where(lane == ln, val, rec)
    route_ref[...] = rec


def _out_proj(sb, y, x, sbw, w_out, g, nw, sc, sh, tm, rows_per_stream, h2_dtype, router=None):
    t = x.shape[0]
    mr = g.shape[1]
    row = lambda n: pl.BlockSpec((tm, n), lambda i: (i, 0))
    full = lambda a: pl.BlockSpec(a.shape, lambda i: (0,) * a.ndim)
    ms = _mod_spec(tm, mr, rows_per_stream)
    in_specs = [row(SB_WIDTH), row(SSD_INNER), row(D_MODEL), full(sbw), full(w_out), ms, full(nw), ms, ms]
    out_specs = [row(D_MODEL), row(D_MODEL)]
    out_shape = [jax.ShapeDtypeStruct((t, D_MODEL), F32), jax.ShapeDtypeStruct((t, D_MODEL), h2_dtype)]
    args = [sb, y, x, sbw, w_out, g, nw, sc, sh]
    body = _outproj_kernel
    if router is not None:
        body = _outproj_router_kernel
        args += list(router)
        in_specs += [full(a) for a in router]
        out_specs += [row(LANES), row(LANES), pl.BlockSpec((1, LANES), lambda i: (0, 0))]
        out_shape += [jax.ShapeDtypeStruct((t, LANES), F32), jax.ShapeDtypeStruct((t, LANES), F32),
                      jax.ShapeDtypeStruct((1, LANES), F32)]
    return pl.pallas_call(
        body,
        grid=(t // tm,),
        in_specs=in_specs,
        out_specs=out_specs,
        out_shape=out_shape,
        compiler_params=_cparams(("arbitrary",)),
        name="out_proj" if router is None else "out_proj_router",
    )(*args)


def _ffn_kernel(h_ref, wg_ref, wu_ref, wd_ref, x_ref, g_ref, *rest, n_exp):
    gates_ref = rest[0] if n_exp > 1 else None
    o_ref, acc_ref = rest[-2:]
    e = pl.program_id(1)
    j = pl.program_id(2)

    @pl.when((e == 0) & (j == 0))
    def _():
        acc_ref[...] = jnp.zeros_like(acc_ref)

    part = _swiglu_partial(h_ref[...], wg_ref, wu_ref, wd_ref)
    if n_exp > 1:
        lane = lax.broadcasted_iota(jnp.int32, gates_ref.shape, 1)
        part = part * jnp.sum(jnp.where(lane == e, gates_ref[...], 0.0), axis=-1, keepdims=True)
    acc_ref[...] += part

    @pl.when((e == n_exp - 1) & (j == pl.num_programs(2) - 1))
    def _():
        o_ref[...] = x_ref[...] + g_ref[0] * acc_ref[...]


def _ffn(h, wg, wu, wd, x, g, gates, tm, tf, rows_per_stream):
    t = x.shape[0]
    n_exp = wg.shape[0]
    mr = g.shape[1]
    if mr == 1:
        gspec = pl.BlockSpec((1, 1, D_MODEL), lambda i, e, j: ((i * tm) // rows_per_stream, 0, 0))
    else:
        gspec = pl.BlockSpec((1, tm, D_MODEL), lambda i, e, j: (i, 0, 0))
    row = lambda n: pl.BlockSpec((tm, n), lambda i, e, j: (i, 0))
    in_specs = [row(D_MODEL),
                pl.BlockSpec((1, D_MODEL, tf), lambda i, e, j: (e, 0, j)),
                pl.BlockSpec((1, D_MODEL, tf), lambda i, e, j: (e, 0, j)),
                pl.BlockSpec((1, tf, D_MODEL), lambda i, e, j: (e, j, 0)),
                row(D_MODEL), gspec]
    args = [h, wg, wu, wd, x, g]
    if n_exp > 1:
        in_specs.append(row(LANES))
        args.append(gates)
    return pl.pallas_call(
        functools.partial(_ffn_kernel, n_exp=n_exp),
        grid=(t // tm, n_exp, D_FF // tf),
        in_specs=in_specs,
        out_specs=row(D_MODEL),
        out_shape=jax.ShapeDtypeStruct((t, D_MODEL), F32),
        scratch_shapes=[pltpu.VMEM((tm, D_MODEL), F32)],
        compiler_params=_cparams(("arbitrary", "arbitrary", "arbitrary")),
        name="ffn_dense" if n_exp == 1 else "ffn_experts",
    )(*args)


EXPERT_TILE = 512


def _dispatch_kernel(pos_ref, h_ref, zeros_hbm, xs_hbm, sem, *, tm):
    del zeros_hbm

    def copy(r, slot):
        return pltpu.make_async_copy(h_ref.at[pl.ds(r, 1)], xs_hbm.at[pl.ds(pos_ref[0, 0, slot * tm + r], 1)], sem)

    def issue(r, carry):
        copy(r, 0).start()
        copy(r, 1).start()
        return carry

    def drain(r, carry):
        copy(r, 0).wait()
        copy(r, 1).wait()
        return carry

    lax.fori_loop(0, tm, issue, 0, unroll=8)
    lax.fori_loop(0, tm, drain, 0, unroll=8)


def _dispatch(h, pos, n_rows, tm):
    t = h.shape[0]
    return pl.pallas_call(
        functools.partial(_dispatch_kernel, tm=tm),
        grid=(t // tm,),
        in_specs=[pl.BlockSpec((1, 1, 2 * tm), lambda i: (i, 0, 0), memory_space=pltpu.SMEM),
                  pl.BlockSpec((tm, D_MODEL), lambda i: (i, 0)), pl.BlockSpec(memory_space=pl.ANY)],
        out_specs=pl.BlockSpec(memory_space=pl.ANY),
        out_shape=jax.ShapeDtypeStruct((n_rows, D_MODEL), F32),
        scratch_shapes=[pltpu.SemaphoreType.DMA(())],
        input_output_aliases={2: 0},
        compiler_params=_cparams(("arbitrary",)),
        name="moe_dispatch",
    )(pos, h, jnp.zeros((n_rows, D_MODEL), F32))


def _swiglu_partial(h, wg_ref, wu_ref, wd_ref):
    gt = jnp.dot(h, wg_ref[0], preferred_element_type=F32)
    up = jnp.dot(h, wu_ref[0], preferred_element_type=F32)
    act = (_silu(gt) * up).astype(BF16)
    return jnp.dot(act, wd_ref[0], preferred_element_type=F32)


def _grouped_ffn_kernel(tile_expert_ref, n_active_ref, x_ref, wg_ref, wu_ref, wd_ref, o_ref):
    del tile_expert_ref
    j = pl.program_id(1)
    active = pl.program_id(0) < n_active_ref[0]

    @pl.when(active & (j == 0))
    def _():
        o_ref[...] = _swiglu_partial(x_ref[...].astype(BF16), wg_ref, wu_ref, wd_ref)

    @pl.when(active & (j > 0))
    def _():
        o_ref[...] += _swiglu_partial(x_ref[...].astype(BF16), wg_ref, wu_ref, wd_ref)

    @pl.when(jnp.logical_not(active) & (j == 0))
    def _():
        o_ref[...] = jnp.zeros_like(o_ref)


def _grouped_ffn(xs, tile_expert, n_active, wg, wu, wd, tf):
    n_rows = xs.shape[0]
    row = pl.BlockSpec((EXPERT_TILE, D_MODEL), lambda i, j, te, na: (i, 0))
    grid_spec = pltpu.PrefetchScalarGridSpec(
        num_scalar_prefetch=2,
        grid=(n_rows // EXPERT_TILE, D_FF // tf),
        in_specs=[row,
                  pl.BlockSpec((1, D_MODEL, tf), lambda i, j, te, na: (te[i], 0, j)),
                  pl.BlockSpec((1, D_MODEL, tf), lambda i, j, te, na: (te[i], 0, j)),
                  pl.BlockSpec((1, tf, D_MODEL), lambda i, j, te, na: (te[i], j, 0))],
        out_specs=row)
    return pl.pallas_call(
        _grouped_ffn_kernel,
        grid_spec=grid_spec,
        out_shape=jax.ShapeDtypeStruct((n_rows, D_MODEL), F32),
        compiler_params=_cparams(("arbitrary", "arbitrary")),
        name="ffn_grouped",
    )(tile_expert, n_active, xs, wg, wu, wd)


def _combine_kernel(pos_ref, nxt_ref, route_ref, x_ref, g_ref, ye_hbm, o_ref, buf_ref, sem_ref, *, tm):
    i = pl.program_id(0)
    n = pl.num_programs(0)
    slot = i % 2

    def copy(idx_ref, r, which, s):
        return pltpu.make_async_copy(ye_hbm.at[pl.ds(idx_ref[0, 0, which * tm + r], 1)],
                                     buf_ref.at[s, pl.ds(which * tm + r, 1)], sem_ref.at[s])

    def fetch(idx_ref, s):
        def issue(r, carry):
            copy(idx_ref, r, 0, s).start()
            copy(idx_ref, r, 1, s).start()
            return carry
        lax.fori_loop(0, tm, issue, 0, unroll=8)

    @pl.when(i == 0)
    def _():
        fetch(pos_ref, 0)

    @pl.when(i + 1 < n)
    def _():
        fetch(nxt_ref, 1 - slot)

    def drain(r, carry):
        copy(pos_ref, r, 0, slot).wait()
        copy(pos_ref, r, 1, slot).wait()
        return carry

    lax.fori_loop(0, tm, drain, 0, unroll=8)
    rec = route_ref[...]
    w1 = rec[:, R_W1:R_W1 + 1]
    w2 = rec[:, R_W2:R_W2 + 1]
    rows = buf_ref[slot]
    o_ref[...] = x_ref[...] + g_ref[0] * (w1 * rows[0:tm] + w2 * rows[tm:2 * tm])


def _combine(ye, pos, route, x, g, tm, rows_per_stream):
    t = x.shape[0]
    n = t // tm
    row = lambda w: pl.BlockSpec((tm, w), lambda i: (i, 0))
    smem = lambda f: pl.BlockSpec((1, 1, 2 * tm), f, memory_space=pltpu.SMEM)
    return pl.pallas_call(
        functools.partial(_combine_kernel, tm=tm),
        grid=(n,),
        in_specs=[smem(lambda i: (i, 0, 0)), smem(lambda i: (jnp.minimum(i + 1, n - 1), 0, 0)),
                  row(LANES), row(D_MODEL), _mod_spec(tm, g.shape[1], rows_per_stream),
                  pl.BlockSpec(memory_space=pl.ANY)],
        out_specs=row(D_MODEL),
        out_shape=jax.ShapeDtypeStruct((t, D_MODEL), F32),
        scratch_shapes=[pltpu.VMEM((2, 2 * tm, D_MODEL), F32), pltpu.SemaphoreType.DMA((2,))],
        compiler_params=_cparams(("arbitrary",)),
        name="moe_combine",
    )(pos, pos, route, x, g, ye)


def _sorted_experts(h2f, route, counts, x1, g_f, wg, wu, wd, tm, rows_per_stream):
    t = h2f.shape[0]
    n_tiles = (2 * t) // EXPERT_TILE + N_EXPERTS
    cnt = counts[0, :N_EXPERTS].astype(jnp.int32)
    padded = ((cnt + EXPERT_TILE - 1) // EXPERT_TILE) * EXPERT_TILE
    ends = jnp.cumsum(padded)
    starts = ends - padded
    experts = jnp.arange(N_EXPERTS, dtype=jnp.int32)

    def slots(e_lane, r_lane):
        e = route[:, e_lane].astype(jnp.int32)
        start = jnp.sum(jnp.where(e[:, None] == experts[None, :], starts[None, :], 0), axis=-1)
        return start + route[:, r_lane].astype(jnp.int32)

    pos = jnp.stack([slots(R_E1, R_RANK1).reshape(t // tm, tm), slots(R_E2, R_RANK2).reshape(t // tm, tm)], axis=1)
    pos = pos.reshape(t // tm, 1, 2 * tm)
    tile_start = jnp.arange(n_tiles, dtype=jnp.int32) * EXPERT_TILE
    n_active = (ends[-1] // EXPERT_TILE).astype(jnp.int32)
    clipped = jnp.minimum(tile_start, ends[-1] - EXPERT_TILE)
    tile_expert = jnp.sum((clipped[:, None] >= ends[None, :]).astype(jnp.int32), axis=-1)
    xs = _dispatch(h2f, pos, n_tiles * EXPERT_TILE, tm)
    ye = _grouped_ffn(xs, tile_expert, n_active.reshape(1), wg, wu, wd, D_FF // 2)
    return _combine(ye, pos, route, x1, g_f, tm, rows_per_stream)


def _layer(x, mod, l, p, consts, kv_states, k_past, v_past, s0, c0, tm, ssd_chunk, attn_blk, expand_mod,
           sorted_experts):
    b, s, _ = x.shape
    t = b * s

    def mod_rows(i):
        m = mod[:, i, :]
        if expand_mod:
            return jnp.repeat(m, s, axis=0).reshape(t // tm, tm, D_MODEL)
        return m.reshape(b, 1, D_MODEL)

    sh_m, sc_m, g_m, sh_f, sc_f, g_f = [mod_rows(i) for i in range(6)]
    xf = x.reshape(t, D_MODEL)
    q, k, v, kf, vf, z, xbc, dt = _in_proj(xf, sc_m, sh_m, p['norm_mix_w'][l], p['w_in'][l], consts['head_ones'],
                                           p['q_norm_w'][l], p['k_norm_w'][l], tm, s, l, p['w_in'].shape[0], kv_states)
    shp = lambda a: a.reshape(b, s, a.shape[-1])
    if k_past is None:
        sb = _attn_prompt(shp(q), shp(k), shp(v), consts['from_ones'], attn_blk)
    else:
        sb = _attn_sample(shp(q), shp(k), shp(v), k_past, v_past, l, consts['from_ones'], attn_blk)
    y, s_new, c_new = _ssd(shp(xbc), shp(z), shp(dt), p['conv_w'][l], p['conv_b'][l], p['dt_bias'][l],
                           p['a_log'][l], p['d_skip'][l], p['ssd_norm_w'][l], consts['head_expand'],
                           s0, c0, ssd_chunk)
    mix_args = (sb.reshape(t, SB_WIDTH), y.reshape(t, SSD_INNER), xf, p['sb_norm_w'][l], p['w_out'][l], g_m,
                p['norm_ffn_w'][l], sc_f, sh_f, tm, s)
    i = l // 2
    tf = D_FF // 2
    if l % 2 == 0:
        x1, h2 = _out_proj(*mix_args, BF16)
        x2 = _ffn(h2, p['w_gate_dense'][i:i + 1], p['w_up_dense'][i:i + 1], p['w_down_dense'][i:i + 1],
                  x1, g_f, None, tm, tf, s)
    else:
        router = (p['w_router_hi'][i], p['w_router_lo'][i], p['b_router'][i], consts['before_ones'][:tm, :tm])
        experts = (p['w_gate_moe'][i], p['w_up_moe'][i], p['w_down_moe'][i])
        if sorted_experts:
            x1, h2, _, route, counts = _out_proj(*mix_args, F32, router)
            x2 = _sorted_experts(h2, route, counts, x1, g_f, *experts, tm // 2, s)
        else:
            x1, h2, gates, _, _ = _out_proj(*mix_args, BF16, router)
            x2 = _ffn(h2, *experts, x1, g_f, gates, tm, tf, s)
    return x2.reshape(b, s, D_MODEL), kf, vf, s_new, c_new


def kernel(x_prompt, x_sample, c_prompt, c_sample, cache_sb_k, cache_sb_v, state_ssd, state_conv, w_mod, b_mod, norm_mix_w, norm_ffn_w, w_in, q_norm_w, k_norm_w, sb_norm_w, conv_w, conv_b, dt_bias, a_log, d_skip, ssd_norm_w, w_out, w_gate_dense, w_up_dense, w_down_dense, w_router, b_router, w_gate_moe, w_up_moe, w_down_moe):
    depth = w_in.shape[0]
    bp, sp, _ = x_prompt.shape
    bs, ss, _ = x_sample.shape
    past = cache_sb_k.shape[2]
    n_moe = w_router.shape[0]

    row = lambda a: a.reshape(depth, 1, a.shape[-1])
    lane_pad = lambda a: jnp.pad(a, ((0, 0), (0, LANES - a.shape[-1]))).reshape(a.shape[0], 1, LANES)
    w_router_pad = jnp.pad(w_router, ((0, 0), (0, 0), (0, LANES - N_EXPERTS)))
    p = {
        'norm_mix_w': row(norm_mix_w), 'norm_ffn_w': row(norm_ffn_w),
        'w_in': jnp.pad(w_in, ((0, 0), (0, 0), (0, IN_MAIN + DT_PAD - w_in.shape[-1]))).astype(BF16),
        'q_norm_w': row(jnp.tile(q_norm_w, (1, N_HEADS))), 'k_norm_w': row(jnp.tile(k_norm_w, (1, N_HEADS))),
        'sb_norm_w': row(sb_norm_w), 'conv_w': conv_w, 'conv_b': row(conv_b),
        'dt_bias': lane_pad(dt_bias), 'a_log': lane_pad(a_log),
        'd_skip': row(jnp.repeat(d_skip, HEAD_DIM, axis=-1)), 'ssd_norm_w': row(ssd_norm_w),
        'w_out': w_out.astype(BF16),
        'w_gate_dense': w_gate_dense.astype(BF16), 'w_up_dense': w_up_dense.astype(BF16),
        'w_down_dense': w_down_dense.astype(BF16),
        'w_router_hi': w_router_pad.astype(BF16),
        'w_router_lo': (w_router_pad - w_router_pad.astype(BF16).astype(F32)).astype(BF16),
        'b_router': jnp.pad(b_router, ((0, 0), (0, LANES - N_EXPERTS)), constant_values=-1e30).reshape(n_moe, 1, LANES),
        'w_gate_moe': w_gate_moe.astype(BF16), 'w_up_moe': w_up_moe.astype(BF16),
        'w_down_moe': w_down_moe.astype(BF16),
    }
    attn_blk = 256
    tm_prompt = 512
    idx = jnp.arange(attn_blk, dtype=jnp.int32)
    wide = jnp.arange(SB_WIDTH, dtype=jnp.int32)
    tok = jnp.arange(tm_prompt, dtype=jnp.int32)
    consts = {
        'from_ones': (idx[:, None] >= idx[None, :]).astype(BF16),
        'before_ones': (tok[None, :] < tok[:, None]).astype(BF16),
        'head_ones': (wide[:, None] // HEAD_DIM == wide[None, :] // HEAD_DIM).astype(BF16),
        'head_expand': (jnp.arange(LANES, dtype=jnp.int32)[:, None] == wide[None, :] // HEAD_DIM).astype(BF16),
    }

    c_all = jnp.concatenate([c_prompt, c_sample], axis=0)
    mod = _modulation(c_all, w_mod, b_mod).reshape(depth, bp + bs, 6, D_MODEL)

    cache_t = lambda a: jnp.transpose(a, (0, 1, 3, 4, 2)).reshape(depth * bs * SB_WIDTH, past)
    kp, vp = cache_t(cache_sb_k), cache_t(cache_sb_v)
    halo_pad = ((0, 0), (0, 0), (CONV_HALO - (CONV_WIDTH - 1), 0), (0, 0))
    conv_in = jnp.pad(state_conv, halo_pad)
    ssd_in = state_ssd.reshape(depth, bs, N_HEADS * HEAD_DIM, D_STATE)
    zero_s = jnp.zeros((bp, N_HEADS * HEAD_DIM, D_STATE), F32)
    zero_c = jnp.zeros((bp, CONV_HALO, CONV_DIM), F32)

    xp, xs = x_prompt, x_sample
    kv_p = kv_s = None
    outs_p, outs_s = [], []
    for l in range(depth):
        xp, k, v, s, c = _layer(xp, mod[l, :bp], l, p, consts, kv_p, None, None, zero_s, zero_c,
                                tm=tm_prompt, ssd_chunk=256, attn_blk=attn_blk, expand_mod=False, sorted_experts=True)
        kv_p = (k, v)
        outs_p.append((s.reshape(bp, N_HEADS, HEAD_DIM, D_STATE), c[:, CONV_HALO - (CONV_WIDTH - 1):]))
        xs, k, v, s, c = _layer(xs, mod[l, bp:], l, p, consts, kv_s, kp, vp, ssd_in[l], conv_in[l],
                                tm=bs * ss, ssd_chunk=ss, attn_blk=attn_blk, expand_mod=True, sorted_experts=False)
        kv_s = (k, v)
        outs_s.append((s.reshape(bs, N_HEADS, HEAD_DIM, D_STATE), c[:, CONV_HALO - (CONV_WIDTH - 1):]))
    stack = lambda outs, i: jnp.stack([o[i] for o in outs])
    heads = lambda a, b, s: a.reshape(depth, b, s, N_HEADS, HEAD_DIM)
    return (xp, xs,
            heads(kv_p[0], bp, sp), heads(kv_p[1], bp, sp), stack(outs_p, 0), stack(outs_p, 1),
            heads(kv_s[0], bs, ss), heads(kv_s[1], bs, ss), stack(outs_s, 0), stack(outs_s, 1))
```

```python
import functools

import jax
import jax.numpy as jnp
from jax import lax
from jax.experimental import pallas as pl
from jax.experimental.pallas import tpu as pltpu

F32 = jnp.float32
BF16 = jnp.bfloat16
HIGHEST = lax.Precision.HIGHEST

EPS = 1e-6
LOG2_E = 1.4426950408889634
D_MODEL = 1024
N_HEADS = 8
HEAD_DIM = 64
SB_WIDTH = 512
SSD_INNER = 512
SSD_GROUPS = 2
D_STATE = 128
CONV_WIDTH = 4
CONV_DIM = 1024
IN_MAIN = 3072
DT_PAD = 128
D_FF = 2816
N_EXPERTS = 8
LANES = 128
CONV_HALO = 8

VMEM_LIMIT = 56 * 1024 * 1024


def _cparams(sem, vmem=VMEM_LIMIT):
    return pltpu.CompilerParams(dimension_semantics=sem, vmem_limit_bytes=vmem)


def _silu(x):
    return x * (1.0 / (1.0 + jnp.exp2(x * -LOG2_E)))


def _softplus(x):
    return jnp.maximum(x, 0.0) + jnp.log(1.0 + jnp.exp(-jnp.abs(x)))


def _rms(x):
    return x * lax.rsqrt(jnp.mean(x * x, axis=-1, keepdims=True) + EPS)


def _mod_kernel(c_ref, w_ref, b_ref, o_ref):
    s = _silu(c_ref[...])
    o_ref[0] = jnp.dot(s, w_ref[0], preferred_element_type=F32, precision=HIGHEST) + b_ref[0]


def _modulation(c_all, w_mod, b_mod):
    depth, _, n6 = w_mod.shape
    ns = c_all.shape[0]
    nj = n6 // D_MODEL
    return pl.pallas_call(
        _mod_kernel,
        grid=(depth, nj),
        in_specs=[pl.BlockSpec((ns, D_MODEL), lambda l, j: (0, 0)),
                  pl.BlockSpec((1, D_MODEL, D_MODEL), lambda l, j: (l, 0, j)),
                  pl.BlockSpec((1, 1, D_MODEL), lambda l, j: (l, 0, j))],
        out_specs=pl.BlockSpec((1, ns, D_MODEL), lambda l, j: (l, 0, j)),
        out_shape=jax.ShapeDtypeStruct((depth, ns, n6), F32),
        compiler_params=_cparams(("arbitrary", "arbitrary")),
        name="adaln_mod",
    )(c_all, w_mod, b_mod.reshape(depth, 1, n6))


def _inproj_kernel(x_ref, sc_ref, sh_ref, nw_ref, w_ref, bd_ref, qw_ref, kw_ref, *rest, nt):
    kf_ref, vf_ref = rest[-5:-3]

    @pl.when(pl.program_id(0) < nt)
    def _():
        _inproj_tile(x_ref, sc_ref, sh_ref, nw_ref, w_ref, bd_ref, qw_ref, kw_ref, *rest[-8:])

    @pl.when(pl.program_id(0) >= nt)
    def _():
        kf_ref[...] = jnp.zeros_like(kf_ref)
        vf_ref[...] = jnp.zeros_like(vf_ref)


def _inproj_tile(x_ref, sc_ref, sh_ref, nw_ref, w_ref, bd_ref, qw_ref, kw_ref,
                 q_ref, k_ref, v_ref, kf_ref, vf_ref, z_ref, xbc_ref, dt_ref):
    h = _rms(x_ref[...]) * nw_ref[...]
    h = h * (1.0 + sc_ref[0]) + sh_ref[0]
    proj = jnp.dot(h.astype(BF16), w_ref[...], preferred_element_type=F32)

    def head_norm(t, w):
        ss = jnp.dot((t * t).astype(BF16), bd_ref[...], preferred_element_type=F32)
        return t * lax.rsqrt(ss * (1.0 / HEAD_DIM) + EPS) * w

    q = head_norm(proj[:, 0:SB_WIDTH], qw_ref[...])
    k = head_norm(proj[:, SB_WIDTH:2 * SB_WIDTH], kw_ref[...])
    v = proj[:, 2 * SB_WIDTH:3 * SB_WIDTH]
    q_ref[...] = (q * (HEAD_DIM ** -0.5 * LOG2_E)).astype(BF16)
    k_ref[...] = k.astype(BF16)
    v_ref[...] = v.astype(BF16)
    for hd in range(N_HEADS):
        cols = slice(hd * HEAD_DIM, (hd + 1) * HEAD_DIM)
        kf_ref[pl.ds(hd, k.shape[0], stride=N_HEADS), :] = k[:, cols]
        vf_ref[pl.ds(hd, v.shape[0], stride=N_HEADS), :] = v[:, cols]
    z_ref[...] = proj[:, 3 * SB_WIDTH:3 * SB_WIDTH + SSD_INNER].astype(BF16)
    xbc_ref[...] = proj[:, 3 * SB_WIDTH + SSD_INNER:IN_MAIN]
    dt_ref[...] = proj[:, IN_MAIN:IN_MAIN + DT_PAD]


def _mod_spec(tm, mr, rows_per_stream, tile=lambda i: i):
    if mr == 1:
        return pl.BlockSpec((1, 1, D_MODEL), lambda i: ((tile(i) * tm) // rows_per_stream, 0, 0))
    return pl.BlockSpec((1, tm, D_MODEL), lambda i: (tile(i), 0, 0))


def _in_proj(x, sc, sh, nw, w_pad, bd, qw, kw, tm, rows_per_stream, layer, depth, kv_states):
    t = x.shape[0]
    mr = sc.shape[1]
    nt = t // tm
    first = kv_states is None
    assert first == (layer == 0)
    tile = lambda i: jnp.minimum(i, nt - 1)
    row = lambda n: pl.BlockSpec((tm, n), lambda i: (tile(i), 0))
    full = lambda a: pl.BlockSpec(a.shape, lambda i: (0,) * a.ndim)
    state = pl.BlockSpec((tm * N_HEADS, HEAD_DIM), lambda i: (layer * nt + i, 0))
    state_shape = jax.ShapeDtypeStruct((depth * t * N_HEADS, HEAD_DIM), F32)
    outs = [(SB_WIDTH, BF16), (SB_WIDTH, BF16), (SB_WIDTH, BF16), None, None,
            (SSD_INNER, BF16), (CONV_DIM, F32), (DT_PAD, F32)]
    in_specs = [row(D_MODEL), _mod_spec(tm, mr, rows_per_stream, tile), _mod_spec(tm, mr, rows_per_stream, tile),
                full(nw), full(w_pad), full(bd), full(qw), full(kw)]
    args = [x, sc, sh, nw, w_pad, bd, qw, kw]
    aliases = {}
    if not first:
        in_specs += [pl.BlockSpec(memory_space=pl.ANY)] * 2
        aliases = {len(args): 3, len(args) + 1: 4}
        args += list(kv_states)
    return pl.pallas_call(
        functools.partial(_inproj_kernel, nt=nt),
        grid=(depth * nt if first else nt,),
        in_specs=in_specs,
        out_specs=[state if o is None else row(o[0]) for o in outs],
        out_shape=[state_shape if o is None else jax.ShapeDtypeStruct((t, o[0]), o[1]) for o in outs],
        input_output_aliases=aliases,
        compiler_params=_cparams(("arbitrary",)),
        name="in_proj",
    )(*args)


def _dot_nt(a, b):
    return lax.dot_general(a, b, (((1,), (1,)), ((), ())), preferred_element_type=F32)


def _sb_accumulate(blocks, from_ones, acc_ref, c_ref):
    acc = acc_ref[...]
    c = c_ref[...]
    for z, weighted, mask in blocks:
        u = jnp.maximum(z, 0.0) + jnp.log(1.0 + jnp.exp2(-jnp.abs(z))) * LOG2_E
        if mask is not None:
            u = jnp.where(mask, u, 0.0)
        cum = jnp.dot(u.astype(BF16), from_ones, preferred_element_type=F32)
        a = jnp.exp2(z - cum)
        if mask is not None:
            a = jnp.where(mask, a, 0.0)
        acc = acc + jnp.exp2(-c) * weighted(a.astype(BF16))
        c = c + cum[:, 0:1]
    acc_ref[...] = acc
    c_ref[...] = c


def _sb_blocks(qs, blocks, from_ones, acc_ref, c_ref):
    _sb_accumulate([(_dot_nt(qs, kblk), functools.partial(jnp.dot, b=vblk, preferred_element_type=F32), mask)
                    for kblk, vblk, mask in blocks], from_ones, acc_ref, c_ref)


def _stack_heads(q):
    lane = lax.broadcasted_iota(jnp.int32, q.shape, 1)
    zero = jnp.zeros_like(q)
    return jnp.concatenate([jnp.where(lane < HEAD_DIM, q, zero), jnp.where(lane >= HEAD_DIM, q, zero)], axis=0)


def _unstack_heads(acc, tq):
    lane = lax.broadcasted_iota(jnp.int32, (tq, LANES), 1)
    return jnp.where(lane < HEAD_DIM, acc[0:tq], acc[tq:2 * tq])


def _strict_lower(n, m):
    r = lax.broadcasted_iota(jnp.int32, (n, m), 0)
    c = lax.broadcasted_iota(jnp.int32, (n, m), 1)
    return c < r


def _stacked_causal(tq, tk):
    r = lax.broadcasted_iota(jnp.int32, (2 * tq, tk), 0)
    c = lax.broadcasted_iota(jnp.int32, (2 * tq, tk), 1)
    return c < jnp.where(r >= tq, r - tq, r)


GROUP = 4


def _attn_prompt_kernel(q_ref, k_ref, v_ref, m_ref, o_ref, acc_ref, c_ref, *, blk):
    i = pl.program_id(2)
    qs = _stack_heads(q_ref[0])
    acc_ref[...] = jnp.zeros_like(acc_ref)
    c_ref[...] = jnp.zeros_like(c_ref)
    from_ones = m_ref[...]
    diag_mask = _stacked_causal(blk, blk)

    def kv(kb, mask=None):
        start = pl.multiple_of(kb * blk, blk)
        return k_ref[0, pl.ds(start, blk), :], v_ref[0, pl.ds(start, blk), :], mask

    def run(top, n, first_mask=None):
        blocks = [kv(top - d, first_mask if d == 0 else None) for d in range(n)]
        _sb_blocks(qs, blocks, from_ones, acc_ref, c_ref)

    lead = jnp.minimum(i + 1, GROUP)
    for n in range(1, GROUP + 1):
        @pl.when(lead == n)
        def _(n=n):
            run(i, n, diag_mask)

    rest = i + 1 - lead
    full = rest // GROUP

    def body(j, carry):
        run(rest - 1 - GROUP * j, GROUP)
        return carry

    lax.fori_loop(0, full, body, 0)
    tail = rest - GROUP * full
    for n in range(1, GROUP):
        @pl.when(tail == n)
        def _(n=n):
            run(n - 1, n)

    o_ref[0] = _unstack_heads(acc_ref[...], blk).astype(o_ref.dtype)


def _attn_prompt(q, k, v, from_ones, blk):
    b, s, _ = q.shape
    npair = SB_WIDTH // LANES
    qspec = pl.BlockSpec((1, blk, LANES), lambda bi, p, i: (bi, i, p))
    kvspec = pl.BlockSpec((1, s, LANES), lambda bi, p, i: (bi, 0, p))
    return pl.pallas_call(
        functools.partial(_attn_prompt_kernel, blk=blk),
        grid=(b, npair, s // blk),
        in_specs=[qspec, kvspec, kvspec, pl.BlockSpec(from_ones.shape, lambda bi, p, i: (0, 0))],
        out_specs=qspec,
        out_shape=jax.ShapeDtypeStruct((b, s, SB_WIDTH), BF16),
        scratch_shapes=[pltpu.VMEM((2 * blk, LANES), F32), pltpu.VMEM((2 * blk, 1), F32)],
        compiler_params=_cparams(("arbitrary", "arbitrary", "arbitrary")),
        name="sb_attn_prompt",
    )(q, k, v, from_ones)


PAST_BLOCKS_PER_STEP = 2


def _attn_sample_kernel(q_ref, kn_ref, vn_ref, kp_ref, vp_ref, m_ref, o_ref, qbd_ref, acc_ref, c_ref, *, blk):
    j = pl.program_id(1)
    tq = q_ref.shape[1]
    rows = N_HEADS * tq
    from_ones = m_ref[...]
    row_head = lax.broadcasted_iota(jnp.int32, (rows, SB_WIDTH), 0) // tq
    lane_head = lax.broadcasted_iota(jnp.int32, (rows, SB_WIDTH), 1) // HEAD_DIM

    @pl.when(j == 0)
    def _():
        q = q_ref[0]
        stacked = jnp.concatenate([q] * N_HEADS, axis=0)
        qbd_ref[...] = jnp.where(row_head == lane_head, stacked, jnp.zeros_like(stacked))
        acc_ref[...] = jnp.zeros_like(acc_ref)
        c_ref[...] = jnp.zeros_like(c_ref)
        r = lax.broadcasted_iota(jnp.int32, (rows, blk), 0)
        mask = lax.broadcasted_iota(jnp.int32, (rows, blk), 1) < r - (r // tq) * tq
        block = (_dot_nt(qbd_ref[...], kn_ref[0]), functools.partial(jnp.dot, b=vn_ref[0], preferred_element_type=F32),
                 mask)
        _sb_accumulate([block], from_ones, acc_ref, c_ref)

    @pl.when(j > 0)
    def _():
        blocks = []
        for sub in reversed(range(PAST_BLOCKS_PER_STEP)):
            cols = slice(sub * blk, (sub + 1) * blk)
            z = jnp.dot(qbd_ref[...], kp_ref[:, cols].astype(BF16), preferred_element_type=F32)
            blocks.append((z, functools.partial(_dot_nt, b=vp_ref[:, cols].astype(BF16)), None))
        _sb_accumulate(blocks, from_ones, acc_ref, c_ref)

    @pl.when(j == pl.num_programs(1) - 1)
    def _():
        kept = jnp.where(row_head == lane_head, acc_ref[...], 0.0)
        out = kept[0:tq]
        for hd in range(1, N_HEADS):
            out = out + kept[hd * tq:(hd + 1) * tq]
        o_ref[0] = out.astype(o_ref.dtype)


def _attn_sample(q, k_new, v_new, k_past_t, v_past_t, layer, from_ones, blk):
    b, tq, _ = q.shape
    past = k_past_t.shape[1]
    span = PAST_BLOCKS_PER_STEP * blk
    nsteps = past // span
    assert nsteps * span == past
    nspec = pl.BlockSpec((1, tq, SB_WIDTH), lambda bi, j: (bi, 0, 0))
    kspec = pl.BlockSpec((1, blk, SB_WIDTH), lambda bi, j: (bi, 0, 0))
    pspec = pl.BlockSpec((SB_WIDTH, span), lambda bi, j: (layer * b + bi, nsteps - jnp.maximum(j, 1)))
    pad = ((0, 0), (0, blk - tq), (0, 0))
    k_new, v_new = jnp.pad(k_new, pad), jnp.pad(v_new, pad)
    rows = N_HEADS * tq
    return pl.pallas_call(
        functools.partial(_attn_sample_kernel, blk=blk),
        grid=(b, 1 + nsteps),
        in_specs=[nspec, kspec, kspec, pspec, pspec, pl.BlockSpec(from_ones.shape, lambda bi, j: (0, 0))],
        out_specs=nspec,
        out_shape=jax.ShapeDtypeStruct((b, tq, SB_WIDTH), BF16),
        scratch_shapes=[pltpu.VMEM((rows, SB_WIDTH), BF16), pltpu.VMEM((rows, SB_WIDTH), F32),
                        pltpu.VMEM((rows, 1), F32)],
        compiler_params=_cparams(("arbitrary", "arbitrary")),
        name="sb_attn_sample",
    )(q, k_new, v_new, k_past_t, v_past_t, from_ones)


def _ssd_kernel(xbc_ref, z_ref, dt_ref, cw_ref, cb_ref, dtb_ref, alog_ref, dsk_ref, nw_ref, exp_ref,
                s0_ref, c0_ref, y_ref, sT_out_ref, cs_out_ref, ext_ref, st_ref, *, L):
    ci = pl.program_id(1)
    nc = pl.num_programs(1)
    gw = SSD_INNER // SSD_GROUPS
    hpg = N_HEADS // SSD_GROUPS

    @pl.when(ci == 0)
    def _():
        ext_ref[0:CONV_HALO, :] = c0_ref[0]
        st_ref[...] = s0_ref[0].T

    ext_ref[CONV_HALO:CONV_HALO + L, :] = xbc_ref[0]
    conv = cb_ref[...]
    for w in range(CONV_WIDTH):
        off = CONV_HALO - (CONV_WIDTH - 1) + w
        conv = conv + ext_ref[off:off + L, :] * cw_ref[w:w + 1, :]
    xbc = _silu(conv)
    tail = ext_ref[L:L + CONV_HALO, :]
    ext_ref[0:CONV_HALO, :] = tail

    x = xbc[:, 0:SSD_INNER]
    bm = xbc[:, SSD_INNER:SSD_INNER + SSD_GROUPS * D_STATE]
    cm = xbc[:, SSD_INNER + SSD_GROUPS * D_STATE:]
    dt = _softplus(dt_ref[0] + dtb_ref[...])
    da = dt * (-jnp.exp(alog_ref[...]))
    tril = jnp.where(_strict_lower(L, L) | (lax.broadcasted_iota(jnp.int32, (L, L), 0)
                                            == lax.broadcasted_iota(jnp.int32, (L, L), 1)), 1.0, 0.0)
    a_cum = jnp.dot(tril, da * LOG2_E, preferred_element_type=F32, precision=HIGHEST)
    a_cum_t = a_cum.T
    a_last = a_cum[L - 1:L, :]
    expand = exp_ref[...]

    def bcast(t):
        hi = t.astype(BF16)
        lo = (t - hi.astype(F32)).astype(BF16)
        return (jnp.dot(hi, expand, preferred_element_type=F32) + jnp.dot(lo, expand, preferred_element_type=F32))

    per_head = bcast(jnp.concatenate([dt, jnp.exp2(a_cum), jnp.exp2(a_last - a_cum)], axis=0))
    dt_e, ea_e, ds_e = per_head[0:L], per_head[L:2 * L], per_head[2 * L:3 * L]
    xdt = x * dt_e
    wgt = (xdt * ds_e).astype(BF16)
    xdt_b = xdt.astype(BF16)
    bm_t = bm.T.astype(BF16)
    bm_b = bm.astype(BF16)
    cm_b = cm.astype(BF16)
    causal = _strict_lower(L, L) | (lax.broadcasted_iota(jnp.int32, (L, L), 0)
                                    == lax.broadcasted_iota(jnp.int32, (L, L), 1))
    lane_g = lax.broadcasted_iota(jnp.int32, (L, gw), 1)
    y_parts = []
    for g in range(SSD_GROUPS):
        gs = slice(g * gw, (g + 1) * gw)
        ns = slice(g * D_STATE, (g + 1) * D_STATE)
        cb = lax.dot_general(cm_b[:, ns], bm_b[:, ns], (((1,), (1,)), ((), ())), preferred_element_type=F32)
        st_g = st_ref[:, gs]
        y_g = jnp.dot(cm_b[:, ns], st_g.astype(BF16), preferred_element_type=F32) * ea_e[:, gs]
        for hl in range(hpg):
            hd = g * hpg + hl
            seg = a_cum[:, hd:hd + 1] - a_cum_t[hd:hd + 1, :]
            decay = jnp.where(causal, jnp.exp2(seg), 0.0)
            scores = (cb * decay).astype(BF16)
            xh = jnp.where((lane_g >= hl * HEAD_DIM) & (lane_g < (hl + 1) * HEAD_DIM), xdt_b[:, gs],
                           jnp.zeros_like(xdt_b[:, gs]))
            y_g = y_g + jnp.dot(scores, xh, preferred_element_type=F32)
        y_parts.append(y_g)
        new_states = jnp.dot(bm_t[ns, :], wgt[:, gs], preferred_element_type=F32)
        st_ref[:, gs] = st_g * ea_e[L - 1:L, gs] + new_states
    y = jnp.concatenate(y_parts, axis=-1) + dsk_ref[...] * x
    y = y * _silu(z_ref[0].astype(F32))
    y = jnp.concatenate([_rms(y[:, g * gw:(g + 1) * gw]) for g in range(SSD_GROUPS)], axis=-1) * nw_ref[...]
    y_ref[0] = y.astype(y_ref.dtype)

    @pl.when(ci == nc - 1)
    def _():
        sT_out_ref[0] = st_ref[...].T
        cs_out_ref[0] = tail


def _ssd(xbc, z, dt, cw, cb, dtb, alog, dsk_e, nw, expand, s0, c0, L):
    b, s, _ = xbc.shape
    seq = lambda n: pl.BlockSpec((1, L, n), lambda bi, ci: (bi, ci, 0))
    full = lambda a: pl.BlockSpec(a.shape, lambda bi, ci: (0,) * a.ndim)
    per_b = lambda a: pl.BlockSpec((1,) + a.shape[1:], lambda bi, ci: (bi,) + (0,) * (a.ndim - 1))
    return pl.pallas_call(
        functools.partial(_ssd_kernel, L=L),
        grid=(b, s // L),
        in_specs=[seq(CONV_DIM), seq(SSD_INNER), seq(DT_PAD), full(cw), full(cb), full(dtb), full(alog),
                  full(dsk_e), full(nw), full(expand), per_b(s0), per_b(c0)],
        out_specs=[seq(SSD_INNER), per_b(s0), per_b(c0)],
        out_shape=[jax.ShapeDtypeStruct((b, s, SSD_INNER), BF16),
                   jax.ShapeDtypeStruct(s0.shape, F32), jax.ShapeDtypeStruct(c0.shape, F32)],
        scratch_shapes=[pltpu.VMEM((L + CONV_HALO, CONV_DIM), F32), pltpu.VMEM((D_STATE, SSD_INNER), F32)],
        compiler_params=_cparams(("arbitrary", "arbitrary")),
        name="ssd_mixer",
    )(xbc, z, dt, cw, cb, dtb, alog, dsk_e, nw, expand, s0, c0)


def _mix_residual_norm(sb_ref, y_ref, x_ref, sbw_ref, w_ref, g_ref, nw_ref, sc_ref, sh_ref):
    sbn = _rms(sb_ref[...].astype(F32)) * sbw_ref[...]
    cat = jnp.concatenate([sbn.astype(BF16), y_ref[...]], axis=-1)
    mix = jnp.dot(cat, w_ref[...], preferred_element_type=F32)
    x1 = x_ref[...] + g_ref[0] * mix
    h2 = _rms(x1) * nw_ref[...]
    return x1, h2 * (1.0 + sc_ref[0]) + sh_ref[0]


def _outproj_kernel(sb_ref, y_ref, x_ref, sbw_ref, w_ref, g_ref, nw_ref, sc_ref, sh_ref, x1_ref, h2_ref):
    x1, h2 = _mix_residual_norm(sb_ref, y_ref, x_ref, sbw_ref, w_ref, g_ref, nw_ref, sc_ref, sh_ref)
    x1_ref[...] = x1
    h2_ref[...] = h2.astype(h2_ref.dtype)


R_E1, R_E2, R_W1, R_W2, R_RANK1, R_RANK2 = range(6)


def _outproj_router_kernel(sb_ref, y_ref, x_ref, sbw_ref, w_ref, g_ref, nw_ref, sc_ref, sh_ref,
                           wrh_ref, wrl_ref, br_ref, tri_ref,
                           x1_ref, h2_ref, gates_ref, route_ref, count_ref):
    x1, h2 = _mix_residual_norm(sb_ref, y_ref, x_ref, sbw_ref, w_ref, g_ref, nw_ref, sc_ref, sh_ref)
    x1_ref[...] = x1
    h2_ref[...] = h2.astype(h2_ref.dtype)

    @pl.when(pl.program_id(0) == 0)
    def _():
        count_ref[...] = jnp.zeros_like(count_ref)

    h_hi = h2.astype(BF16)
    h_lo = (h2 - h_hi.astype(F32)).astype(BF16)
    dot = lambda a, b: jnp.dot(a, b, preferred_element_type=F32)
    logits = dot(h_hi, wrh_ref[...]) + (dot(h_lo, wrh_ref[...]) + dot(h_hi, wrl_ref[...])) + br_ref[...]
    lane = lax.broadcasted_iota(jnp.int32, logits.shape, 1)
    m1 = jnp.max(logits, axis=-1, keepdims=True)
    i1 = jnp.min(jnp.where(logits == m1, lane, LANES), axis=-1, keepdims=True)
    rest = jnp.where(lane == i1, -jnp.inf, logits)
    m2 = jnp.max(rest, axis=-1, keepdims=True)
    i2 = jnp.min(jnp.where(rest == m2, lane, LANES), axis=-1, keepdims=True)
    w1 = 1.0 / (1.0 + jnp.exp(m2 - m1))
    w2 = 1.0 - w1
    hot1 = lane == i1
    hot2 = lane == i2
    gates_ref[...] = jnp.where(hot1, w1, 0.0) + jnp.where(hot2, w2, 0.0)
    hot = jnp.where(hot1 | hot2, 1.0, 0.0)
    before = dot(tri_ref[...], hot.astype(BF16)) + count_ref[...]
    rank1 = jnp.sum(jnp.where(hot1, before, 0.0), axis=-1, keepdims=True)
    rank2 = jnp.sum(jnp.where(hot2, before, 0.0), axis=-1, keepdims=True)
    count_ref[...] += jnp.sum(hot, axis=0, keepdims=True)
    rec = jnp.zeros(logits.shape, F32)
    for ln, val in ((R_E1, i1.astype(F32)), (R_E2, i2.astype(F32)), (R_W1, w1), (R_W2, w2),
                    (R_RANK1, rank1), (R_RANK2, rank2)):
        rec = jnp.where(lane == ln, val, rec)
    route_ref[...] = rec


def _out_proj(sb, y, x, sbw, w_out, g, nw, sc, sh, tm, rows_per_stream, h2_dtype, router=None):
    t = x.shape[0]
    mr = g.shape[1]
    row = lambda n: pl.BlockSpec((tm, n), lambda i: (i, 0))
    full = lambda a: pl.BlockSpec(a.shape, lambda i: (0,) * a.ndim)
    ms = _mod_spec(tm, mr, rows_per_stream)
    in_specs = [row(SB_WIDTH), row(SSD_INNER), row(D_MODEL), full(sbw), full(w_out), ms, full(nw), ms, ms]
    out_specs = [row(D_MODEL), row(D_MODEL)]
    out_shape = [jax.ShapeDtypeStruct((t, D_MODEL), F32), jax.ShapeDtypeStruct((t, D_MODEL), h2_dtype)]
    args = [sb, y, x, sbw, w_out, g, nw, sc, sh]
    body = _outproj_kernel
    if router is not None:
        body = _outproj_router_kernel
        args += list(router)
        in_specs += [full(a) for a in router]
        out_specs += [row(LANES), row(LANES), pl.BlockSpec((1, LANES), lambda i: (0, 0))]
        out_shape += [jax.ShapeDtypeStruct((t, LANES), F32), jax.ShapeDtypeStruct((t, LANES), F32),
                      jax.ShapeDtypeStruct((1, LANES), F32)]
    return pl.pallas_call(
        body,
        grid=(t // tm,),
        in_specs=in_specs,
        out_specs=out_specs,
        out_shape=out_shape,
        compiler_params=_cparams(("arbitrary",)),
        name="out_proj" if router is None else "out_proj_router",
    )(*args)


def _ffn_kernel(h_ref, wg_ref, wu_ref, wd_ref, x_ref, g_ref, *rest, n_exp):
    gates_ref = rest[0] if n_exp > 1 else None
    o_ref, acc_ref = rest[-2:]
    e = pl.program_id(1)
    j = pl.program_id(2)

    @pl.when((e == 0) & (j == 0))
    def _():
        acc_ref[...] = jnp.zeros_like(acc_ref)

    part = _swiglu_partial(h_ref[...], wg_ref, wu_ref, wd_ref)
    if n_exp > 1:
        lane = lax.broadcasted_iota(jnp.int32, gates_ref.shape, 1)
        part = part * jnp.sum(jnp.where(lane == e, gates_ref[...], 0.0), axis=-1, keepdims=True)
    acc_ref[...] += part

    @pl.when((e == n_exp - 1) & (j == pl.num_programs(2) - 1))
    def _():
        o_ref[...] = x_ref[...] + g_ref[0] * acc_ref[...]


def _ffn(h, wg, wu, wd, x, g, gates, tm, tf, rows_per_stream):
    t = x.shape[0]
    n_exp = wg.shape[0]
    mr = g.shape[1]
    if mr == 1:
        gspec = pl.BlockSpec((1, 1, D_MODEL), lambda i, e, j: ((i * tm) // rows_per_stream, 0, 0))
    else:
        gspec = pl.BlockSpec((1, tm, D_MODEL), lambda i, e, j: (i, 0, 0))
    row = lambda n: pl.BlockSpec((tm, n), lambda i, e, j: (i, 0))
    in_specs = [row(D_MODEL),
                pl.BlockSpec((1, D_MODEL, tf), lambda i, e, j: (e, 0, j)),
                pl.BlockSpec((1, D_MODEL, tf), lambda i, e, j: (e, 0, j)),
                pl.BlockSpec((1, tf, D_MODEL), lambda i, e, j: (e, j, 0)),
                row(D_MODEL), gspec]
    args = [h, wg, wu, wd, x, g]
    if n_exp > 1:
        in_specs.append(row(LANES))
        args.append(gates)
    return pl.pallas_call(
        functools.partial(_ffn_kernel, n_exp=n_exp),
        grid=(t // tm, n_exp, D_FF // tf),
        in_specs=in_specs,
        out_specs=row(D_MODEL),
        out_shape=jax.ShapeDtypeStruct((t, D_MODEL), F32),
        scratch_shapes=[pltpu.VMEM((tm, D_MODEL), F32)],
        compiler_params=_cparams(("arbitrary", "arbitrary", "arbitrary")),
        name="ffn_dense" if n_exp == 1 else "ffn_experts",
    )(*args)


EXPERT_TILE = 512


def _dispatch_kernel(pos_ref, h_ref, zeros_hbm, xs_hbm, sem, *, tm):
    del zeros_hbm

    def copy(r, slot):
        return pltpu.make_async_copy(h_ref.at[pl.ds(r, 1)], xs_hbm.at[pl.ds(pos_ref[0, 0, slot * tm + r], 1)], sem)

    def issue(r, carry):
        copy(r, 0).start()
        copy(r, 1).start()
        return carry

    def drain(r, carry):
        copy(r, 0).wait()
        copy(r, 1).wait()
        return carry

    lax.fori_loop(0, tm, issue, 0, unroll=8)
    lax.fori_loop(0, tm, drain, 0, unroll=8)


def _dispatch(h, pos, n_rows, tm):
    t = h.shape[0]
    return pl.pallas_call(
        functools.partial(_dispatch_kernel, tm=tm),
        grid=(t // tm,),
        in_specs=[pl.BlockSpec((1, 1, 2 * tm), lambda i: (i, 0, 0), memory_space=pltpu.SMEM),
                  pl.BlockSpec((tm, D_MODEL), lambda i: (i, 0)), pl.BlockSpec(memory_space=pl.ANY)],
        out_specs=pl.BlockSpec(memory_space=pl.ANY),
        out_shape=jax.ShapeDtypeStruct((n_rows, D_MODEL), F32),
        scratch_shapes=[pltpu.SemaphoreType.DMA(())],
        input_output_aliases={2: 0},
        compiler_params=_cparams(("arbitrary",)),
        name="moe_dispatch",
    )(pos, h, jnp.zeros((n_rows, D_MODEL), F32))


def _swiglu_partial(h, wg_ref, wu_ref, wd_ref):
    gt = jnp.dot(h, wg_ref[0], preferred_element_type=F32)
    up = jnp.dot(h, wu_ref[0], preferred_element_type=F32)
    act = (_silu(gt) * up).astype(BF16)
    return jnp.dot(act, wd_ref[0], preferred_element_type=F32)


def _grouped_ffn_kernel(tile_expert_ref, n_active_ref, x_ref, wg_ref, wu_ref, wd_ref, o_ref):
    del tile_expert_ref
    j = pl.program_id(1)
    active = pl.program_id(0) < n_active_ref[0]

    @pl.when(active & (j == 0))
    def _():
        o_ref[...] = _swiglu_partial(x_ref[...].astype(BF16), wg_ref, wu_ref, wd_ref)

    @pl.when(active & (j > 0))
    def _():
        o_ref[...] += _swiglu_partial(x_ref[...].astype(BF16), wg_ref, wu_ref, wd_ref)

    @pl.when(jnp.logical_not(active) & (j == 0))
    def _():
        o_ref[...] = jnp.zeros_like(o_ref)


def _grouped_ffn(xs, tile_expert, n_active, wg, wu, wd, tf):
    n_rows = xs.shape[0]
    row = pl.BlockSpec((EXPERT_TILE, D_MODEL), lambda i, j, te, na: (i, 0))
    grid_spec = pltpu.PrefetchScalarGridSpec(
        num_scalar_prefetch=2,
        grid=(n_rows // EXPERT_TILE, D_FF // tf),
        in_specs=[row,
                  pl.BlockSpec((1, D_MODEL, tf), lambda i, j, te, na: (te[i], 0, j)),
                  pl.BlockSpec((1, D_MODEL, tf), lambda i, j, te, na: (te[i], 0, j)),
                  pl.BlockSpec((1, tf, D_MODEL), lambda i, j, te, na: (te[i], j, 0))],
        out_specs=row)
    return pl.pallas_call(
        _grouped_ffn_kernel,
        grid_spec=grid_spec,
        out_shape=jax.ShapeDtypeStruct((n_rows, D_MODEL), F32),
        compiler_params=_cparams(("arbitrary", "arbitrary")),
        name="ffn_grouped",
    )(tile_expert, n_active, xs, wg, wu, wd)


def _combine_kernel(pos_ref, nxt_ref, route_ref, x_ref, g_ref, ye_hbm, o_ref, buf_ref, sem_ref, *, tm):
    i = pl.program_id(0)
    n = pl.num_programs(0)
    slot = i % 2

    def copy(idx_ref, r, which, s):
        return pltpu.make_async_copy(ye_hbm.at[pl.ds(idx_ref[0, 0, which * tm + r], 1)],
                                     buf_ref.at[s, pl.ds(which * tm + r, 1)], sem_ref.at[s])

    def fetch(idx_ref, s):
        def issue(r, carry):
            copy(idx_ref, r, 0, s).start()
            copy(idx_ref, r, 1, s).start()
            return carry
        lax.fori_loop(0, tm, issue, 0, unroll=8)

    @pl.when(i == 0)
    def _():
        fetch(pos_ref, 0)

    @pl.when(i + 1 < n)
    def _():
        fetch(nxt_ref, 1 - slot)

    def drain(r, carry):
        copy(pos_ref, r, 0, slot).wait()
        copy(pos_ref, r, 1, slot).wait()
        return carry

    lax.fori_loop(0, tm, drain, 0, unroll=8)
    rec = route_ref[...]
    w1 = rec[:, R_W1:R_W1 + 1]
    w2 = rec[:, R_W2:R_W2 + 1]
    rows = buf_ref[slot]
    o_ref[...] = x_ref[...] + g_ref[0] * (w1 * rows[0:tm] + w2 * rows[tm:2 * tm])


def _combine(ye, pos, route, x, g, tm, rows_per_stream):
    t = x.shape[0]
    n = t // tm
    row = lambda w: pl.BlockSpec((tm, w), lambda i: (i, 0))
    smem = lambda f: pl.BlockSpec((1, 1, 2 * tm), f, memory_space=pltpu.SMEM)
    return pl.pallas_call(
        functools.partial(_combine_kernel, tm=tm),
        grid=(n,),
        in_specs=[smem(lambda i: (i, 0, 0)), smem(lambda i: (jnp.minimum(i + 1, n - 1), 0, 0)),
                  row(LANES), row(D_MODEL), _mod_spec(tm, g.shape[1], rows_per_stream),
                  pl.BlockSpec(memory_space=pl.ANY)],
        out_specs=row(D_MODEL),
        out_shape=jax.ShapeDtypeStruct((t, D_MODEL), F32),
        scratch_shapes=[pltpu.VMEM((2, 2 * tm, D_MODEL), F32), pltpu.SemaphoreType.DMA((2,))],
        compiler_params=_cparams(("arbitrary",)),
        name="moe_combine",
    )(pos, pos, route, x, g, ye)


def _sorted_experts(h2f, route, counts, x1, g_f, wg, wu, wd, tm, rows_per_stream):
    t = h2f.shape[0]
    n_tiles = (2 * t) // EXPERT_TILE + N_EXPERTS
    cnt = counts[0, :N_EXPERTS].astype(jnp.int32)
    padded = ((cnt + EXPERT_TILE - 1) // EXPERT_TILE) * EXPERT_TILE
    ends = jnp.cumsum(padded)
    starts = ends - padded
    experts = jnp.arange(N_EXPERTS, dtype=jnp.int32)

    def slots(e_lane, r_lane):
        e = route[:, e_lane].astype(jnp.int32)
        start = jnp.sum(jnp.where(e[:, None] == experts[None, :], starts[None, :], 0), axis=-1)
        return start + route[:, r_lane].astype(jnp.int32)

    pos = jnp.stack([slots(R_E1, R_RANK1).reshape(t // tm, tm), slots(R_E2, R_RANK2).reshape(t // tm, tm)], axis=1)
    pos = pos.reshape(t // tm, 1, 2 * tm)
    tile_start = jnp.arange(n_tiles, dtype=jnp.int32) * EXPERT_TILE
    n_active = (ends[-1] // EXPERT_TILE).astype(jnp.int32)
    clipped = jnp.minimum(tile_start, ends[-1] - EXPERT_TILE)
    tile_expert = jnp.sum((clipped[:, None] >= ends[None, :]).astype(jnp.int32), axis=-1)
    xs = _dispatch(h2f, pos, n_tiles * EXPERT_TILE, tm)
    ye = _grouped_ffn(xs, tile_expert, n_active.reshape(1), wg, wu, wd, D_FF // 2)
    return _combine(ye, pos, route, x1, g_f, tm, rows_per_stream)


def _layer(x, mod, l, p, consts, kv_states, k_past, v_past, s0, c0, tm, ssd_chunk, attn_blk, expand_mod,
           sorted_experts):
    b, s, _ = x.shape
    t = b * s

    def mod_rows(i):
        m = mod[:, i, :]
        if expand_mod:
            return jnp.repeat(m, s, axis=0).reshape(t // tm, tm, D_MODEL)
        return m.reshape(b, 1, D_MODEL)

    sh_m, sc_m, g_m, sh_f, sc_f, g_f = [mod_rows(i) for i in range(6)]
    xf = x.reshape(t, D_MODEL)
    q, k, v, kf, vf, z, xbc, dt = _in_proj(xf, sc_m, sh_m, p['norm_mix_w'][l], p['w_in'][l], consts['head_ones'],
                                           p['q_norm_w'][l], p['k_norm_w'][l], tm, s, l, p['w_in'].shape[0], kv_states)
    shp = lambda a: a.reshape(b, s, a.shape[-1])
    if k_past is None:
        sb = _attn_prompt(shp(q), shp(k), shp(v), consts['from_ones'], attn_blk)
    else:
        sb = _attn_sample(shp(q), shp(k), shp(v), k_past, v_past, l, consts['from_ones'], attn_blk)
    y, s_new, c_new = _ssd(shp(xbc), shp(z), shp(dt), p['conv_w'][l], p['conv_b'][l], p['dt_bias'][l],
                           p['a_log'][l], p['d_skip'][l], p['ssd_norm_w'][l], consts['head_expand'],
                           s0, c0, ssd_chunk)
    mix_args = (sb.reshape(t, SB_WIDTH), y.reshape(t, SSD_INNER), xf, p['sb_norm_w'][l], p['w_out'][l], g_m,
                p['norm_ffn_w'][l], sc_f, sh_f, tm, s)
    i = l // 2
    tf = D_FF // 2
    if l % 2 == 0:
        x1, h2 = _out_proj(*mix_args, BF16)
        x2 = _ffn(h2, p['w_gate_dense'][i:i + 1], p['w_up_dense'][i:i + 1], p['w_down_dense'][i:i + 1],
                  x1, g_f, None, tm, tf, s)
    else:
        router = (p['w_router_hi'][i], p['w_router_lo'][i], p['b_router'][i], consts['before_ones'][:tm, :tm])
        experts = (p['w_gate_moe'][i], p['w_up_moe'][i], p['w_down_moe'][i])
        if sorted_experts:
            x1, h2, _, route, counts = _out_proj(*mix_args, F32, router)
            x2 = _sorted_experts(h2, route, counts, x1, g_f, *experts, tm, s)
        else:
            x1, h2, gates, _, _ = _out_proj(*mix_args, BF16, router)
            x2 = _ffn(h2, *experts, x1, g_f, gates, tm, tf, s)
    return x2.reshape(b, s, D_MODEL), kf, vf, s_new, c_new


def kernel(x_prompt, x_sample, c_prompt, c_sample, cache_sb_k, cache_sb_v, state_ssd, state_conv, w_mod, b_mod, norm_mix_w, norm_ffn_w, w_in, q_norm_w, k_norm_w, sb_norm_w, conv_w, conv_b, dt_bias, a_log, d_skip, ssd_norm_w, w_out, w_gate_dense, w_up_dense, w_down_dense, w_router, b_router, w_gate_moe, w_up_moe, w_down_moe):
    depth = w_in.shape[0]
    bp, sp, _ = x_prompt.shape
    bs, ss, _ = x_sample.shape
    past = cache_sb_k.shape[2]
    n_moe = w_router.shape[0]

    row = lambda a: a.reshape(depth, 1, a.shape[-1])
    lane_pad = lambda a: jnp.pad(a, ((0, 0), (0, LANES - a.shape[-1]))).reshape(a.shape[0], 1, LANES)
    w_router_pad = jnp.pad(w_router, ((0, 0), (0, 0), (0, LANES - N_EXPERTS)))
    p = {
        'norm_mix_w': row(norm_mix_w), 'norm_ffn_w': row(norm_ffn_w),
        'w_in': jnp.pad(w_in, ((0, 0), (0, 0), (0, IN_MAIN + DT_PAD - w_in.shape[-1]))).astype(BF16),
        'q_norm_w': row(jnp.tile(q_norm_w, (1, N_HEADS))), 'k_norm_w': row(jnp.tile(k_norm_w, (1, N_HEADS))),
        'sb_norm_w': row(sb_norm_w), 'conv_w': conv_w, 'conv_b': row(conv_b),
        'dt_bias': lane_pad(dt_bias), 'a_log': lane_pad(a_log),
        'd_skip': row(jnp.repeat(d_skip, HEAD_DIM, axis=-1)), 'ssd_norm_w': row(ssd_norm_w),
        'w_out': w_out.astype(BF16),
        'w_gate_dense': w_gate_dense.astype(BF16), 'w_up_dense': w_up_dense.astype(BF16),
        'w_down_dense': w_down_dense.astype(BF16),
        'w_router_hi': w_router_pad.astype(BF16),
        'w_router_lo': (w_router_pad - w_router_pad.astype(BF16).astype(F32)).astype(BF16),
        'b_router': jnp.pad(b_router, ((0, 0), (0, LANES - N_EXPERTS)), constant_values=-1e30).reshape(n_moe, 1, LANES),
        'w_gate_moe': w_gate_moe.astype(BF16), 'w_up_moe': w_up_moe.astype(BF16),
        'w_down_moe': w_down_moe.astype(BF16),
    }
    attn_blk = 256
    tm_prompt = 512
    idx = jnp.arange(attn_blk, dtype=jnp.int32)
    wide = jnp.arange(SB_WIDTH, dtype=jnp.int32)
    tok = jnp.arange(tm_prompt, dtype=jnp.int32)
    consts = {
        'from_ones': (idx[:, None] >= idx[None, :]).astype(BF16),
        'before_ones': (tok[None, :] < tok[:, None]).astype(BF16),
        'head_ones': (wide[:, None] // HEAD_DIM == wide[None, :] // HEAD_DIM).astype(BF16),
        'head_expand': (jnp.arange(LANES, dtype=jnp.int32)[:, None] == wide[None, :] // HEAD_DIM).astype(BF16),
    }

    c_all = jnp.concatenate([c_prompt, c_sample], axis=0)
    mod = _modulation(c_all, w_mod, b_mod).reshape(depth, bp + bs, 6, D_MODEL)

    cache_t = lambda a: jnp.transpose(a, (0, 1, 3, 4, 2)).reshape(depth * bs * SB_WIDTH, past)
    kp, vp = cache_t(cache_sb_k), cache_t(cache_sb_v)
    halo_pad = ((0, 0), (0, 0), (CONV_HALO - (CONV_WIDTH - 1), 0), (0, 0))
    conv_in = jnp.pad(state_conv, halo_pad)
    ssd_in = state_ssd.reshape(depth, bs, N_HEADS * HEAD_DIM, D_STATE)
    zero_s = jnp.zeros((bp, N_HEADS * HEAD_DIM, D_STATE), F32)
    zero_c = jnp.zeros((bp, CONV_HALO, CONV_DIM), F32)

    xp, xs = x_prompt, x_sample
    kv_p = kv_s = None
    outs_p, outs_s = [], []
    for l in range(depth):
        xp, k, v, s, c = _layer(xp, mod[l, :bp], l, p, consts, kv_p, None, None, zero_s, zero_c,
                                tm=tm_prompt, ssd_chunk=256, attn_blk=attn_blk, expand_mod=False, sorted_experts=True)
        kv_p = (k, v)
        outs_p.append((s.reshape(bp, N_HEADS, HEAD_DIM, D_STATE), c[:, CONV_HALO - (CONV_WIDTH - 1):]))
        xs, k, v, s, c = _layer(xs, mod[l, bp:], l, p, consts, kv_s, kp, vp, ssd_in[l], conv_in[l],
                                tm=bs * ss, ssd_chunk=ss, attn_blk=attn_blk, expand_mod=True, sorted_experts=False)
        kv_s = (k, v)
        outs_s.append((s.reshape(bs, N_HEADS, HEAD_DIM, D_STATE), c[:, CONV_HALO - (CONV_WIDTH - 1):]))
    stack = lambda outs, i: jnp.stack([o[i] for o in outs])
    heads = lambda a, b, s: a.reshape(depth, b, s, N_HEADS, HEAD_DIM)
    return (xp, xs,
            heads(kv_p[0], bp, sp), heads(kv_p[1], bp, sp), stack(outs_p, 0), stack(outs_p, 1),
            heads(kv_s[0], bs, ss), heads(kv_s[1], bs, ss), stack(outs_s, 0), stack(outs_s, 1))
```

```python
import functools

import jax
import jax.numpy as jnp
from jax import lax
from jax.experimental import pallas as pl
from jax.experimental.pallas import tpu as pltpu

F32 = jnp.float32
BF16 = jnp.bfloat16
HIGHEST = lax.Precision.HIGHEST

EPS = 1e-6
LOG2_E = 1.4426950408889634
D_MODEL = 1024
N_HEADS = 8
HEAD_DIM = 64
SB_WIDTH = 512
SSD_INNER = 512
SSD_GROUPS = 2
D_STATE = 128
CONV_WIDTH = 4
CONV_DIM = 1024
IN_MAIN = 3072
DT_PAD = 128
D_FF = 2816
N_EXPERTS = 8
LANES = 128
CONV_HALO = 8

VMEM_LIMIT = 56 * 1024 * 1024


def _cparams(sem, vmem=VMEM_LIMIT):
    return pltpu.CompilerParams(dimension_semantics=sem, vmem_limit_bytes=vmem)


def _silu(x):
    return x * (1.0 / (1.0 + jnp.exp2(x * -LOG2_E)))


def _softplus(x):
    return jnp.maximum(x, 0.0) + jnp.log(1.0 + jnp.exp(-jnp.abs(x)))


def _rms(x):
    return x * lax.rsqrt(jnp.mean(x * x, axis=-1, keepdims=True) + EPS)


def _mod_kernel(c_ref, w_ref, b_ref, o_ref):
    s = _silu(c_ref[...])
    o_ref[0] = jnp.dot(s, w_ref[0], preferred_element_type=F32, precision=HIGHEST) + b_ref[0]


def _modulation(c_all, w_mod, b_mod):
    depth, _, n6 = w_mod.shape
    ns = c_all.shape[0]
    nj = n6 // D_MODEL
    return pl.pallas_call(
        _mod_kernel,
        grid=(depth, nj),
        in_specs=[pl.BlockSpec((ns, D_MODEL), lambda l, j: (0, 0)),
                  pl.BlockSpec((1, D_MODEL, D_MODEL), lambda l, j: (l, 0, j)),
                  pl.BlockSpec((1, 1, D_MODEL), lambda l, j: (l, 0, j))],
        out_specs=pl.BlockSpec((1, ns, D_MODEL), lambda l, j: (l, 0, j)),
        out_shape=jax.ShapeDtypeStruct((depth, ns, n6), F32),
        compiler_params=_cparams(("arbitrary", "arbitrary")),
        name="adaln_mod",
    )(c_all, w_mod, b_mod.reshape(depth, 1, n6))


def _inproj_kernel(x_ref, sc_ref, sh_ref, nw_ref, w_ref, bd_ref, qw_ref, kw_ref, *rest, nt):
    kf_ref, vf_ref = rest[-5:-3]

    @pl.when(pl.program_id(0) < nt)
    def _():
        _inproj_tile(x_ref, sc_ref, sh_ref, nw_ref, w_ref, bd_ref, qw_ref, kw_ref, *rest[-8:])

    @pl.when(pl.program_id(0) >= nt)
    def _():
        kf_ref[...] = jnp.zeros_like(kf_ref)
        vf_ref[...] = jnp.zeros_like(vf_ref)


def _inproj_tile(x_ref, sc_ref, sh_ref, nw_ref, w_ref, bd_ref, qw_ref, kw_ref,
                 q_ref, k_ref, v_ref, kf_ref, vf_ref, z_ref, xbc_ref, dt_ref):
    h = _rms(x_ref[...]) * nw_ref[...]
    h = h * (1.0 + sc_ref[0]) + sh_ref[0]
    proj = jnp.dot(h.astype(BF16), w_ref[...], preferred_element_type=F32)

    def head_norm(t, w):
        ss = jnp.dot((t * t).astype(BF16), bd_ref[...], preferred_element_type=F32)
        return t * lax.rsqrt(ss * (1.0 / HEAD_DIM) + EPS) * w

    q = head_norm(proj[:, 0:SB_WIDTH], qw_ref[...])
    k = head_norm(proj[:, SB_WIDTH:2 * SB_WIDTH], kw_ref[...])
    v = proj[:, 2 * SB_WIDTH:3 * SB_WIDTH]
    q_ref[...] = (q * (HEAD_DIM ** -0.5 * LOG2_E)).astype(BF16)
    k_ref[...] = k.astype(BF16)
    v_ref[...] = v.astype(BF16)
    for hd in range(N_HEADS):
        cols = slice(hd * HEAD_DIM, (hd + 1) * HEAD_DIM)
        kf_ref[pl.ds(hd, k.shape[0], stride=N_HEADS), :] = k[:, cols]
        vf_ref[pl.ds(hd, v.shape[0], stride=N_HEADS), :] = v[:, cols]
    z_ref[...] = proj[:, 3 * SB_WIDTH:3 * SB_WIDTH + SSD_INNER].astype(BF16)
    xbc_ref[...] = proj[:, 3 * SB_WIDTH + SSD_INNER:IN_MAIN]
    dt_ref[...] = proj[:, IN_MAIN:IN_MAIN + DT_PAD]


def _mod_spec(tm, mr, rows_per_stream, tile=lambda i: i):
    if mr == 1:
        return pl.BlockSpec((1, 1, D_MODEL), lambda i: ((tile(i) * tm) // rows_per_stream, 0, 0))
    return pl.BlockSpec((1, tm, D_MODEL), lambda i: (tile(i), 0, 0))


def _in_proj(x, sc, sh, nw, w_pad, bd, qw, kw, tm, rows_per_stream, layer, depth, kv_states):
    t = x.shape[0]
    mr = sc.shape[1]
    nt = t // tm
    first = kv_states is None
    assert first == (layer == 0)
    tile = lambda i: jnp.minimum(i, nt - 1)
    row = lambda n: pl.BlockSpec((tm, n), lambda i: (tile(i), 0))
    full = lambda a: pl.BlockSpec(a.shape, lambda i: (0,) * a.ndim)
    state = pl.BlockSpec((tm * N_HEADS, HEAD_DIM), lambda i: (layer * nt + i, 0))
    state_shape = jax.ShapeDtypeStruct((depth * t * N_HEADS, HEAD_DIM), F32)
    outs = [(SB_WIDTH, BF16), (SB_WIDTH, BF16), (SB_WIDTH, BF16), None, None,
            (SSD_INNER, BF16), (CONV_DIM, F32), (DT_PAD, F32)]
    in_specs = [row(D_MODEL), _mod_spec(tm, mr, rows_per_stream, tile), _mod_spec(tm, mr, rows_per_stream, tile),
                full(nw), full(w_pad), full(bd), full(qw), full(kw)]
    args = [x, sc, sh, nw, w_pad, bd, qw, kw]
    aliases = {}
    if not first:
        in_specs += [pl.BlockSpec(memory_space=pl.ANY)] * 2
        aliases = {len(args): 3, len(args) + 1: 4}
        args += list(kv_states)
    return pl.pallas_call(
        functools.partial(_inproj_kernel, nt=nt),
        grid=(depth * nt if first else nt,),
        in_specs=in_specs,
        out_specs=[state if o is None else row(o[0]) for o in outs],
        out_shape=[state_shape if o is None else jax.ShapeDtypeStruct((t, o[0]), o[1]) for o in outs],
        input_output_aliases=aliases,
        compiler_params=_cparams(("arbitrary",)),
        name="in_proj",
    )(*args)


def _dot_nt(a, b):
    return lax.dot_general(a, b, (((1,), (1,)), ((), ())), preferred_element_type=F32)


def _sb_accumulate(blocks, from_ones, acc_ref, c_ref):
    acc = acc_ref[...]
    c = c_ref[...]
    for z, weighted, mask in blocks:
        u = jnp.maximum(z, 0.0) + jnp.log(1.0 + jnp.exp2(-jnp.abs(z))) * LOG2_E
        if mask is not None:
            u = jnp.where(mask, u, 0.0)
        cum = jnp.dot(u.astype(BF16), from_ones, preferred_element_type=F32)
        a = jnp.exp2(z - cum)
        if mask is not None:
            a = jnp.where(mask, a, 0.0)
        acc = acc + jnp.exp2(-c) * weighted(a.astype(BF16))
        c = c + cum[:, 0:1]
    acc_ref[...] = acc
    c_ref[...] = c


def _sb_blocks(qs, blocks, from_ones, acc_ref, c_ref):
    _sb_accumulate([(_dot_nt(qs, kblk), functools.partial(jnp.dot, b=vblk, preferred_element_type=F32), mask)
                    for kblk, vblk, mask in blocks], from_ones, acc_ref, c_ref)


def _stack_heads(q):
    lane = lax.broadcasted_iota(jnp.int32, q.shape, 1)
    zero = jnp.zeros_like(q)
    return jnp.concatenate([jnp.where(lane < HEAD_DIM, q, zero), jnp.where(lane >= HEAD_DIM, q, zero)], axis=0)


def _unstack_heads(acc, tq):
    lane = lax.broadcasted_iota(jnp.int32, (tq, LANES), 1)
    return jnp.where(lane < HEAD_DIM, acc[0:tq], acc[tq:2 * tq])


def _strict_lower(n, m):
    r = lax.broadcasted_iota(jnp.int32, (n, m), 0)
    c = lax.broadcasted_iota(jnp.int32, (n, m), 1)
    return c < r


def _stacked_causal(tq, tk):
    r = lax.broadcasted_iota(jnp.int32, (2 * tq, tk), 0)
    c = lax.broadcasted_iota(jnp.int32, (2 * tq, tk), 1)
    return c < jnp.where(r >= tq, r - tq, r)


GROUP = 6


def _attn_prompt_kernel(q_ref, k_ref, v_ref, m_ref, o_ref, acc_ref, c_ref, *, blk):
    i = pl.program_id(2)
    qs = _stack_heads(q_ref[0])
    acc_ref[...] = jnp.zeros_like(acc_ref)
    c_ref[...] = jnp.zeros_like(c_ref)
    from_ones = m_ref[...]
    diag_mask = _stacked_causal(blk, blk)

    def kv(kb, mask=None):
        start = pl.multiple_of(kb * blk, blk)
        return k_ref[0, pl.ds(start, blk), :], v_ref[0, pl.ds(start, blk), :], mask

    def run(top, n, first_mask=None):
        blocks = [kv(top - d, first_mask if d == 0 else None) for d in range(n)]
        _sb_blocks(qs, blocks, from_ones, acc_ref, c_ref)

    lead = jnp.minimum(i + 1, GROUP)
    for n in range(1, GROUP + 1):
        @pl.when(lead == n)
        def _(n=n):
            run(i, n, diag_mask)

    rest = i + 1 - lead
    full = rest // GROUP

    def body(j, carry):
        run(rest - 1 - GROUP * j, GROUP)
        return carry

    lax.fori_loop(0, full, body, 0)
    tail = rest - GROUP * full
    for n in range(1, GROUP):
        @pl.when(tail == n)
        def _(n=n):
            run(n - 1, n)

    o_ref[0] = _unstack_heads(acc_ref[...], blk).astype(o_ref.dtype)


def _attn_prompt(q, k, v, from_ones, blk):
    b, s, _ = q.shape
    npair = SB_WIDTH // LANES
    qspec = pl.BlockSpec((1, blk, LANES), lambda bi, p, i: (bi, i, p))
    kvspec = pl.BlockSpec((1, s, LANES), lambda bi, p, i: (bi, 0, p))
    return pl.pallas_call(
        functools.partial(_attn_prompt_kernel, blk=blk),
        grid=(b, npair, s // blk),
        in_specs=[qspec, kvspec, kvspec, pl.BlockSpec(from_ones.shape, lambda bi, p, i: (0, 0))],
        out_specs=qspec,
        out_shape=jax.ShapeDtypeStruct((b, s, SB_WIDTH), BF16),
        scratch_shapes=[pltpu.VMEM((2 * blk, LANES), F32), pltpu.VMEM((2 * blk, 1), F32)],
        compiler_params=_cparams(("arbitrary", "arbitrary", "arbitrary")),
        name="sb_attn_prompt",
    )(q, k, v, from_ones)


PAST_BLOCKS_PER_STEP = 2


def _attn_sample_kernel(q_ref, kn_ref, vn_ref, kp_ref, vp_ref, m_ref, o_ref, qbd_ref, acc_ref, c_ref, *, blk):
    j = pl.program_id(1)
    tq = q_ref.shape[1]
    rows = N_HEADS * tq
    from_ones = m_ref[...]
    row_head = lax.broadcasted_iota(jnp.int32, (rows, SB_WIDTH), 0) // tq
    lane_head = lax.broadcasted_iota(jnp.int32, (rows, SB_WIDTH), 1) // HEAD_DIM

    @pl.when(j == 0)
    def _():
        q = q_ref[0]
        stacked = jnp.concatenate([q] * N_HEADS, axis=0)
        qbd_ref[...] = jnp.where(row_head == lane_head, stacked, jnp.zeros_like(stacked))
        acc_ref[...] = jnp.zeros_like(acc_ref)
        c_ref[...] = jnp.zeros_like(c_ref)
        r = lax.broadcasted_iota(jnp.int32, (rows, blk), 0)
        mask = lax.broadcasted_iota(jnp.int32, (rows, blk), 1) < r - (r // tq) * tq
        block = (_dot_nt(qbd_ref[...], kn_ref[0]), functools.partial(jnp.dot, b=vn_ref[0], preferred_element_type=F32),
                 mask)
        _sb_accumulate([block], from_ones, acc_ref, c_ref)

    @pl.when(j > 0)
    def _():
        blocks = []
        for sub in reversed(range(PAST_BLOCKS_PER_STEP)):
            cols = slice(sub * blk, (sub + 1) * blk)
            z = jnp.dot(qbd_ref[...], kp_ref[:, cols].astype(BF16), preferred_element_type=F32)
            blocks.append((z, functools.partial(_dot_nt, b=vp_ref[:, cols].astype(BF16)), None))
        _sb_accumulate(blocks, from_ones, acc_ref, c_ref)

    @pl.when(j == pl.num_programs(1) - 1)
    def _():
        kept = jnp.where(row_head == lane_head, acc_ref[...], 0.0)
        out = kept[0:tq]
        for hd in range(1, N_HEADS):
            out = out + kept[hd * tq:(hd + 1) * tq]
        o_ref[0] = out.astype(o_ref.dtype)


def _attn_sample(q, k_new, v_new, k_past_t, v_past_t, layer, from_ones, blk):
    b, tq, _ = q.shape
    past = k_past_t.shape[1]
    span = PAST_BLOCKS_PER_STEP * blk
    nsteps = past // span
    assert nsteps * span == past
    nspec = pl.BlockSpec((1, tq, SB_WIDTH), lambda bi, j: (bi, 0, 0))
    kspec = pl.BlockSpec((1, blk, SB_WIDTH), lambda bi, j: (bi, 0, 0))
    pspec = pl.BlockSpec((SB_WIDTH, span), lambda bi, j: (layer * b + bi, nsteps - jnp.maximum(j, 1)))
    pad = ((0, 0), (0, blk - tq), (0, 0))
    k_new, v_new = jnp.pad(k_new, pad), jnp.pad(v_new, pad)
    rows = N_HEADS * tq
    return pl.pallas_call(
        functools.partial(_attn_sample_kernel, blk=blk),
        grid=(b, 1 + nsteps),
        in_specs=[nspec, kspec, kspec, pspec, pspec, pl.BlockSpec(from_ones.shape, lambda bi, j: (0, 0))],
        out_specs=nspec,
        out_shape=jax.ShapeDtypeStruct((b, tq, SB_WIDTH), BF16),
        scratch_shapes=[pltpu.VMEM((rows, SB_WIDTH), BF16), pltpu.VMEM((rows, SB_WIDTH), F32),
                        pltpu.VMEM((rows, 1), F32)],
        compiler_params=_cparams(("arbitrary", "arbitrary")),
        name="sb_attn_sample",
    )(q, k_new, v_new, k_past_t, v_past_t, from_ones)


def _ssd_kernel(xbc_ref, z_ref, dt_ref, cw_ref, cb_ref, dtb_ref, alog_ref, dsk_ref, nw_ref, exp_ref,
                s0_ref, c0_ref, y_ref, sT_out_ref, cs_out_ref, ext_ref, st_ref, *, L):
    ci = pl.program_id(1)
    nc = pl.num_programs(1)
    gw = SSD_INNER // SSD_GROUPS
    hpg = N_HEADS // SSD_GROUPS

    @pl.when(ci == 0)
    def _():
        ext_ref[0:CONV_HALO, :] = c0_ref[0]
        st_ref[...] = s0_ref[0].T

    ext_ref[CONV_HALO:CONV_HALO + L, :] = xbc_ref[0]
    conv = cb_ref[...]
    for w in range(CONV_WIDTH):
        off = CONV_HALO - (CONV_WIDTH - 1) + w
        conv = conv + ext_ref[off:off + L, :] * cw_ref[w:w + 1, :]
    xbc = _silu(conv)
    tail = ext_ref[L:L + CONV_HALO, :]
    ext_ref[0:CONV_HALO, :] = tail

    x = xbc[:, 0:SSD_INNER]
    bm = xbc[:, SSD_INNER:SSD_INNER + SSD_GROUPS * D_STATE]
    cm = xbc[:, SSD_INNER + SSD_GROUPS * D_STATE:]
    dt = _softplus(dt_ref[0] + dtb_ref[...])
    da = dt * (-jnp.exp(alog_ref[...]))
    tril = jnp.where(_strict_lower(L, L) | (lax.broadcasted_iota(jnp.int32, (L, L), 0)
                                            == lax.broadcasted_iota(jnp.int32, (L, L), 1)), 1.0, 0.0)
    a_cum = jnp.dot(tril, da * LOG2_E, preferred_element_type=F32, precision=HIGHEST)
    a_cum_t = a_cum.T
    a_last = a_cum[L - 1:L, :]
    expand = exp_ref[...]

    def bcast(t):
        hi = t.astype(BF16)
        lo = (t - hi.astype(F32)).astype(BF16)
        return (jnp.dot(hi, expand, preferred_element_type=F32) + jnp.dot(lo, expand, preferred_element_type=F32))

    per_head = bcast(jnp.concatenate([dt, jnp.exp2(a_cum), jnp.exp2(a_last - a_cum)], axis=0))
    dt_e, ea_e, ds_e = per_head[0:L], per_head[L:2 * L], per_head[2 * L:3 * L]
    xdt = x * dt_e
    wgt = (xdt * ds_e).astype(BF16)
    xdt_b = xdt.astype(BF16)
    bm_t = bm.T.astype(BF16)
    bm_b = bm.astype(BF16)
    cm_b = cm.astype(BF16)
    causal = _strict_lower(L, L) | (lax.broadcasted_iota(jnp.int32, (L, L), 0)
                                    == lax.broadcasted_iota(jnp.int32, (L, L), 1))
    lane_g = lax.broadcasted_iota(jnp.int32, (L, gw), 1)
    y_parts = []
    for g in range(SSD_GROUPS):
        gs = slice(g * gw, (g + 1) * gw)
        ns = slice(g * D_STATE, (g + 1) * D_STATE)
        cb = lax.dot_general(cm_b[:, ns], bm_b[:, ns], (((1,), (1,)), ((), ())), preferred_element_type=F32)
        st_g = st_ref[:, gs]
        y_g = jnp.dot(cm_b[:, ns], st_g.astype(BF16), preferred_element_type=F32) * ea_e[:, gs]
        for hl in range(hpg):
            hd = g * hpg + hl
            seg = a_cum[:, hd:hd + 1] - a_cum_t[hd:hd + 1, :]
            decay = jnp.where(causal, jnp.exp2(seg), 0.0)
            scores = (cb * decay).astype(BF16)
            xh = jnp.where((lane_g >= hl * HEAD_DIM) & (lane_g < (hl + 1) * HEAD_DIM), xdt_b[:, gs],
                           jnp.zeros_like(xdt_b[:, gs]))
            y_g = y_g + jnp.dot(scores, xh, preferred_element_type=F32)
        y_parts.append(y_g)
        new_states = jnp.dot(bm_t[ns, :], wgt[:, gs], preferred_element_type=F32)
        st_ref[:, gs] = st_g * ea_e[L - 1:L, gs] + new_states
    y = jnp.concatenate(y_parts, axis=-1) + dsk_ref[...] * x
    y = y * _silu(z_ref[0].astype(F32))
    y = jnp.concatenate([_rms(y[:, g * gw:(g + 1) * gw]) for g in range(SSD_GROUPS)], axis=-1) * nw_ref[...]
    y_ref[0] = y.astype(y_ref.dtype)

    @pl.when(ci == nc - 1)
    def _():
        sT_out_ref[0] = st_ref[...].T
        cs_out_ref[0] = tail


def _ssd(xbc, z, dt, cw, cb, dtb, alog, dsk_e, nw, expand, s0, c0, L):
    b, s, _ = xbc.shape
    seq = lambda n: pl.BlockSpec((1, L, n), lambda bi, ci: (bi, ci, 0))
    full = lambda a: pl.BlockSpec(a.shape, lambda bi, ci: (0,) * a.ndim)
    per_b = lambda a: pl.BlockSpec((1,) + a.shape[1:], lambda bi, ci: (bi,) + (0,) * (a.ndim - 1))
    return pl.pallas_call(
        functools.partial(_ssd_kernel, L=L),
        grid=(b, s // L),
        in_specs=[seq(CONV_DIM), seq(SSD_INNER), seq(DT_PAD), full(cw), full(cb), full(dtb), full(alog),
                  full(dsk_e), full(nw), full(expand), per_b(s0), per_b(c0)],
        out_specs=[seq(SSD_INNER), per_b(s0), per_b(c0)],
        out_shape=[jax.ShapeDtypeStruct((b, s, SSD_INNER), BF16),
                   jax.ShapeDtypeStruct(s0.shape, F32), jax.ShapeDtypeStruct(c0.shape, F32)],
        scratch_shapes=[pltpu.VMEM((L + CONV_HALO, CONV_DIM), F32), pltpu.VMEM((D_STATE, SSD_INNER), F32)],
        compiler_params=_cparams(("arbitrary", "arbitrary")),
        name="ssd_mixer",
    )(xbc, z, dt, cw, cb, dtb, alog, dsk_e, nw, expand, s0, c0)


def _mix_residual_norm(sb_ref, y_ref, x_ref, sbw_ref, w_ref, g_ref, nw_ref, sc_ref, sh_ref):
    sbn = _rms(sb_ref[...].astype(F32)) * sbw_ref[...]
    cat = jnp.concatenate([sbn.astype(BF16), y_ref[...]], axis=-1)
    mix = jnp.dot(cat, w_ref[...], preferred_element_type=F32)
    x1 = x_ref[...] + g_ref[0] * mix
    h2 = _rms(x1) * nw_ref[...]
    return x1, h2 * (1.0 + sc_ref[0]) + sh_ref[0]


def _outproj_kernel(sb_ref, y_ref, x_ref, sbw_ref, w_ref, g_ref, nw_ref, sc_ref, sh_ref, x1_ref, h2_ref):
    x1, h2 = _mix_residual_norm(sb_ref, y_ref, x_ref, sbw_ref, w_ref, g_ref, nw_ref, sc_ref, sh_ref)
    x1_ref[...] = x1
    h2_ref[...] = h2.astype(h2_ref.dtype)


R_E1, R_E2, R_W1, R_W2, R_RANK1, R_RANK2 = range(6)


def _outproj_router_kernel(sb_ref, y_ref, x_ref, sbw_ref, w_ref, g_ref, nw_ref, sc_ref, sh_ref,
                           wrh_ref, wrl_ref, br_ref, tri_ref,
                           x1_ref, h2_ref, gates_ref, route_ref, count_ref):
    x1, h2 = _mix_residual_norm(sb_ref, y_ref, x_ref, sbw_ref, w_ref, g_ref, nw_ref, sc_ref, sh_ref)
    x1_ref[...] = x1
    h2_ref[...] = h2.astype(h2_ref.dtype)

    @pl.when(pl.program_id(0) == 0)
    def _():
        count_ref[...] = jnp.zeros_like(count_ref)

    h_hi = h2.astype(BF16)
    h_lo = (h2 - h_hi.astype(F32)).astype(BF16)
    dot = lambda a, b: jnp.dot(a, b, preferred_element_type=F32)
    logits = dot(h_hi, wrh_ref[...]) + (dot(h_lo, wrh_ref[...]) + dot(h_hi, wrl_ref[...])) + br_ref[...]
    lane = lax.broadcasted_iota(jnp.int32, logits.shape, 1)
    m1 = jnp.max(logits, axis=-1, keepdims=True)
    i1 = jnp.min(jnp.where(logits == m1, lane, LANES), axis=-1, keepdims=True)
    rest = jnp.where(lane == i1, -jnp.inf, logits)
    m2 = jnp.max(rest, axis=-1, keepdims=True)
    i2 = jnp.min(jnp.where(rest == m2, lane, LANES), axis=-1, keepdims=True)
    w1 = 1.0 / (1.0 + jnp.exp(m2 - m1))
    w2 = 1.0 - w1
    hot1 = lane == i1
    hot2 = lane == i2
    gates_ref[...] = jnp.where(hot1, w1, 0.0) + jnp.where(hot2, w2, 0.0)
    hot = jnp.where(hot1 | hot2, 1.0, 0.0)
    before = dot(tri_ref[...], hot.astype(BF16)) + count_ref[...]
    rank1 = jnp.sum(jnp.where(hot1, before, 0.0), axis=-1, keepdims=True)
    rank2 = jnp.sum(jnp.where(hot2, before, 0.0), axis=-1, keepdims=True)
    count_ref[...] += jnp.sum(hot, axis=0, keepdims=True)
    rec = jnp.zeros(logits.shape, F32)
    for ln, val in ((R_E1, i1.astype(F32)), (R_E2, i2.astype(F32)), (R_W1, w1), (R_W2, w2),
                    (R_RANK1, rank1), (R_RANK2, rank2)):
        rec = jnp.where(lane == ln, val, rec)
    route_ref[...] = rec


def _out_proj(sb, y, x, sbw, w_out, g, nw, sc, sh, tm, rows_per_stream, h2_dtype, router=None):
    t = x.shape[0]
    mr = g.shape[1]
    row = lambda n: pl.BlockSpec((tm, n), lambda i: (i, 0))
    full = lambda a: pl.BlockSpec(a.shape, lambda i: (0,) * a.ndim)
    ms = _mod_spec(tm, mr, rows_per_stream)
    in_specs = [row(SB_WIDTH), row(SSD_INNER), row(D_MODEL), full(sbw), full(w_out), ms, full(nw), ms, ms]
    out_specs = [row(D_MODEL), row(D_MODEL)]
    out_shape = [jax.ShapeDtypeStruct((t, D_MODEL), F32), jax.ShapeDtypeStruct((t, D_MODEL), h2_dtype)]
    args = [sb, y, x, sbw, w_out, g, nw, sc, sh]
    body = _outproj_kernel
    if router is not None:
        body = _outproj_router_kernel
        args += list(router)
        in_specs += [full(a) for a in router]
        out_specs += [row(LANES), row(LANES), pl.BlockSpec((1, LANES), lambda i: (0, 0))]
        out_shape += [jax.ShapeDtypeStruct((t, LANES), F32), jax.ShapeDtypeStruct((t, LANES), F32),
                      jax.ShapeDtypeStruct((1, LANES), F32)]
    return pl.pallas_call(
        body,
        grid=(t // tm,),
        in_specs=in_specs,
        out_specs=out_specs,
        out_shape=out_shape,
        compiler_params=_cparams(("arbitrary",)),
        name="out_proj" if router is None else "out_proj_router",
    )(*args)


def _ffn_kernel(h_ref, wg_ref, wu_ref, wd_ref, x_ref, g_ref, *rest, n_exp):
    gates_ref = rest[0] if n_exp > 1 else None
    o_ref, acc_ref = rest[-2:]
    e = pl.program_id(1)
    j = pl.program_id(2)

    @pl.when((e == 0) & (j == 0))
    def _():
        acc_ref[...] = jnp.zeros_like(acc_ref)

    part = _swiglu_partial(h_ref[...], wg_ref, wu_ref, wd_ref)
    if n_exp > 1:
        lane = lax.broadcasted_iota(jnp.int32, gates_ref.shape, 1)
        part = part * jnp.sum(jnp.where(lane == e, gates_ref[...], 0.0), axis=-1, keepdims=True)
    acc_ref[...] += part

    @pl.when((e == n_exp - 1) & (j == pl.num_programs(2) - 1))
    def _():
        o_ref[...] = x_ref[...] + g_ref[0] * acc_ref[...]


def _ffn(h, wg, wu, wd, x, g, gates, tm, tf, rows_per_stream):
    t = x.shape[0]
    n_exp = wg.shape[0]
    mr = g.shape[1]
    if mr == 1:
        gspec = pl.BlockSpec((1, 1, D_MODEL), lambda i, e, j: ((i * tm) // rows_per_stream, 0, 0))
    else:
        gspec = pl.BlockSpec((1, tm, D_MODEL), lambda i, e, j: (i, 0, 0))
    row = lambda n: pl.BlockSpec((tm, n), lambda i, e, j: (i, 0))
    in_specs = [row(D_MODEL),
                pl.BlockSpec((1, D_MODEL, tf), lambda i, e, j: (e, 0, j)),
                pl.BlockSpec((1, D_MODEL, tf), lambda i, e, j: (e, 0, j)),
                pl.BlockSpec((1, tf, D_MODEL), lambda i, e, j: (e, j, 0)),
                row(D_MODEL), gspec]
    args = [h, wg, wu, wd, x, g]
    if n_exp > 1:
        in_specs.append(row(LANES))
        args.append(gates)
    return pl.pallas_call(
        functools.partial(_ffn_kernel, n_exp=n_exp),
        grid=(t // tm, n_exp, D_FF // tf),
        in_specs=in_specs,
        out_specs=row(D_MODEL),
        out_shape=jax.ShapeDtypeStruct((t, D_MODEL), F32),
        scratch_shapes=[pltpu.VMEM((tm, D_MODEL), F32)],
        compiler_params=_cparams(("arbitrary", "arbitrary", "arbitrary")),
        name="ffn_dense" if n_exp == 1 else "ffn_experts",
    )(*args)


EXPERT_TILE = 512


def _dispatch_kernel(pos_ref, h_ref, zeros_hbm, xs_hbm, sem, *, tm):
    del zeros_hbm

    def copy(r, slot):
        return pltpu.make_async_copy(h_ref.at[pl.ds(r, 1)], xs_hbm.at[pl.ds(pos_ref[0, 0, slot * tm + r], 1)], sem)

    def issue(r, carry):
        copy(r, 0).start()
        copy(r, 1).start()
        return carry

    def drain(r, carry):
        copy(r, 0).wait()
        copy(r, 1).wait()
        return carry

    lax.fori_loop(0, tm, issue, 0, unroll=8)
    lax.fori_loop(0, tm, drain, 0, unroll=8)


def _dispatch(h, pos, n_rows, tm):
    t = h.shape[0]
    return pl.pallas_call(
        functools.partial(_dispatch_kernel, tm=tm),
        grid=(t // tm,),
        in_specs=[pl.BlockSpec((1, 1, 2 * tm), lambda i: (i, 0, 0), memory_space=pltpu.SMEM),
                  pl.BlockSpec((tm, D_MODEL), lambda i: (i, 0)), pl.BlockSpec(memory_space=pl.ANY)],
        out_specs=pl.BlockSpec(memory_space=pl.ANY),
        out_shape=jax.ShapeDtypeStruct((n_rows, D_MODEL), F32),
        scratch_shapes=[pltpu.SemaphoreType.DMA(())],
        input_output_aliases={2: 0},
        compiler_params=_cparams(("arbitrary",)),
        name="moe_dispatch",
    )(pos, h, jnp.zeros((n_rows, D_MODEL), F32))


def _swiglu_partial(h, wg_ref, wu_ref, wd_ref):
    gt = jnp.dot(h, wg_ref[0], preferred_element_type=F32)
    up = jnp.dot(h, wu_ref[0], preferred_element_type=F32)
    act = (_silu(gt) * up).astype(BF16)
    return jnp.dot(act, wd_ref[0], preferred_element_type=F32)


def _grouped_ffn_kernel(tile_expert_ref, n_active_ref, x_ref, wg_ref, wu_ref, wd_ref, o_ref):
    del tile_expert_ref
    j = pl.program_id(1)
    active = pl.program_id(0) < n_active_ref[0]

    @pl.when(active & (j == 0))
    def _():
        o_ref[...] = _swiglu_partial(x_ref[...].astype(BF16), wg_ref, wu_ref, wd_ref)

    @pl.when(active & (j > 0))
    def _():
        o_ref[...] += _swiglu_partial(x_ref[...].astype(BF16), wg_ref, wu_ref, wd_ref)

    @pl.when(jnp.logical_not(active) & (j == 0))
    def _():
        o_ref[...] = jnp.zeros_like(o_ref)


def _grouped_ffn(xs, tile_expert, n_active, wg, wu, wd, tf):
    n_rows = xs.shape[0]
    row = pl.BlockSpec((EXPERT_TILE, D_MODEL), lambda i, j, te, na: (i, 0))
    grid_spec = pltpu.PrefetchScalarGridSpec(
        num_scalar_prefetch=2,
        grid=(n_rows // EXPERT_TILE, D_FF // tf),
        in_specs=[row,
                  pl.BlockSpec((1, D_MODEL, tf), lambda i, j, te, na: (te[i], 0, j)),
                  pl.BlockSpec((1, D_MODEL, tf), lambda i, j, te, na: (te[i], 0, j)),
                  pl.BlockSpec((1, tf, D_MODEL), lambda i, j, te, na: (te[i], j, 0))],
        out_specs=row)
    return pl.pallas_call(
        _grouped_ffn_kernel,
        grid_spec=grid_spec,
        out_shape=jax.ShapeDtypeStruct((n_rows, D_MODEL), F32),
        compiler_params=_cparams(("arbitrary", "arbitrary")),
        name="ffn_grouped",
    )(tile_expert, n_active, xs, wg, wu, wd)


def _combine_kernel(pos_ref, nxt_ref, route_ref, x_ref, g_ref, ye_hbm, o_ref, buf_ref, sem_ref, *, tm):
    i = pl.program_id(0)
    n = pl.num_programs(0)
    slot = i % 2

    def copy(idx_ref, r, which, s):
        return pltpu.make_async_copy(ye_hbm.at[pl.ds(idx_ref[0, 0, which * tm + r], 1)],
                                     buf_ref.at[s, pl.ds(which * tm + r, 1)], sem_ref.at[s])

    def fetch(idx_ref, s):
        def issue(r, carry):
            copy(idx_ref, r, 0, s).start()
            copy(idx_ref, r, 1, s).start()
            return carry
        lax.fori_loop(0, tm, issue, 0, unroll=8)

    @pl.when(i == 0)
    def _():
        fetch(pos_ref, 0)

    @pl.when(i + 1 < n)
    def _():
        fetch(nxt_ref, 1 - slot)

    def drain(r, carry):
        copy(pos_ref, r, 0, slot).wait()
        copy(pos_ref, r, 1, slot).wait()
        return carry

    lax.fori_loop(0, tm, drain, 0, unroll=8)
    rec = route_ref[...]
    w1 = rec[:, R_W1:R_W1 + 1]
    w2 = rec[:, R_W2:R_W2 + 1]
    rows = buf_ref[slot]
    o_ref[...] = x_ref[...] + g_ref[0] * (w1 * rows[0:tm] + w2 * rows[tm:2 * tm])


def _combine(ye, pos, route, x, g, tm, rows_per_stream):
    t = x.shape[0]
    n = t // tm
    row = lambda w: pl.BlockSpec((tm, w), lambda i: (i, 0))
    smem = lambda f: pl.BlockSpec((1, 1, 2 * tm), f, memory_space=pltpu.SMEM)
    return pl.pallas_call(
        functools.partial(_combine_kernel, tm=tm),
        grid=(n,),
        in_specs=[smem(lambda i: (i, 0, 0)), smem(lambda i: (jnp.minimum(i + 1, n - 1), 0, 0)),
                  row(LANES), row(D_MODEL), _mod_spec(tm, g.shape[1], rows_per_stream),
                  pl.BlockSpec(memory_space=pl.ANY)],
        out_specs=row(D_MODEL),
        out_shape=jax.ShapeDtypeStruct((t, D_MODEL), F32),
        scratch_shapes=[pltpu.VMEM((2, 2 * tm, D_MODEL), F32), pltpu.SemaphoreType.DMA((2,))],
        compiler_params=_cparams(("arbitrary",)),
        name="moe_combine",
    )(pos, pos, route, x, g, ye)


def _sorted_experts(h2f, route, counts, x1, g_f, wg, wu, wd, tm, rows_per_stream):
    t = h2f.shape[0]
    n_tiles = (2 * t) // EXPERT_TILE + N_EXPERTS
    cnt = counts[0, :N_EXPERTS].astype(jnp.int32)
    padded = ((cnt + EXPERT_TILE - 1) // EXPERT_TILE) * EXPERT_TILE
    ends = jnp.cumsum(padded)
    starts = ends - padded
    experts = jnp.arange(N_EXPERTS, dtype=jnp.int32)

    def slots(e_lane, r_lane):
        e = route[:, e_lane].astype(jnp.int32)
        start = jnp.sum(jnp.where(e[:, None] == experts[None, :], starts[None, :], 0), axis=-1)
        return start + route[:, r_lane].astype(jnp.int32)

    pos = jnp.stack([slots(R_E1, R_RANK1).reshape(t // tm, tm), slots(R_E2, R_RANK2).reshape(t // tm, tm)], axis=1)
    pos = pos.reshape(t // tm, 1, 2 * tm)
    tile_start = jnp.arange(n_tiles, dtype=jnp.int32) * EXPERT_TILE
    n_active = (ends[-1] // EXPERT_TILE).astype(jnp.int32)
    clipped = jnp.minimum(tile_start, ends[-1] - EXPERT_TILE)
    tile_expert = jnp.sum((clipped[:, None] >= ends[None, :]).astype(jnp.int32), axis=-1)
    xs = _dispatch(h2f, pos, n_tiles * EXPERT_TILE, tm)
    ye = _grouped_ffn(xs, tile_expert, n_active.reshape(1), wg, wu, wd, D_FF // 2)
    return _combine(ye, pos, route, x1, g_f, tm, rows_per_stream)


def _layer(x, mod, l, p, consts, kv_states, k_past, v_past, s0, c0, tm, ssd_chunk, attn_blk, expand_mod,
           sorted_experts):
    b, s, _ = x.shape
    t = b * s

    def mod_rows(i):
        m = mod[:, i, :]
        if expand_mod:
            return jnp.repeat(m, s, axis=0).reshape(t // tm, tm, D_MODEL)
        return m.reshape(b, 1, D_MODEL)

    sh_m, sc_m, g_m, sh_f, sc_f, g_f = [mod_rows(i) for i in range(6)]
    xf = x.reshape(t, D_MODEL)
    q, k, v, kf, vf, z, xbc, dt = _in_proj(xf, sc_m, sh_m, p['norm_mix_w'][l], p['w_in'][l], consts['head_ones'],
                                           p['q_norm_w'][l], p['k_norm_w'][l], tm, s, l, p['w_in'].shape[0], kv_states)
    shp = lambda a: a.reshape(b, s, a.shape[-1])
    if k_past is None:
        sb = _attn_prompt(shp(q), shp(k), shp(v), consts['from_ones'], attn_blk)
    else:
        sb = _attn_sample(shp(q), shp(k), shp(v), k_past, v_past, l, consts['from_ones'], attn_blk)
    y, s_new, c_new = _ssd(shp(xbc), shp(z), shp(dt), p['conv_w'][l], p['conv_b'][l], p['dt_bias'][l],
                           p['a_log'][l], p['d_skip'][l], p['ssd_norm_w'][l], consts['head_expand'],
                           s0, c0, ssd_chunk)
    mix_args = (sb.reshape(t, SB_WIDTH), y.reshape(t, SSD_INNER), xf, p['sb_norm_w'][l], p['w_out'][l], g_m,
                p['norm_ffn_w'][l], sc_f, sh_f, tm, s)
    i = l // 2
    tf = D_FF // 2
    if l % 2 == 0:
        x1, h2 = _out_proj(*mix_args, BF16)
        x2 = _ffn(h2, p['w_gate_dense'][i:i + 1], p['w_up_dense'][i:i + 1], p['w_down_dense'][i:i + 1],
                  x1, g_f, None, tm, tf, s)
    else:
        router = (p['w_router_hi'][i], p['w_router_lo'][i], p['b_router'][i], consts['before_ones'][:tm, :tm])
        experts = (p['w_gate_moe'][i], p['w_up_moe'][i], p['w_down_moe'][i])
        if sorted_experts:
            x1, h2, _, route, counts = _out_proj(*mix_args, F32, router)
            x2 = _sorted_experts(h2, route, counts, x1, g_f, *experts, tm, s)
        else:
            x1, h2, gates, _, _ = _out_proj(*mix_args, BF16, router)
            x2 = _ffn(h2, *experts, x1, g_f, gates, tm, tf, s)
    return x2.reshape(b, s, D_MODEL), kf, vf, s_new, c_new


def kernel(x_prompt, x_sample, c_prompt, c_sample, cache_sb_k, cache_sb_v, state_ssd, state_conv, w_mod, b_mod, norm_mix_w, norm_ffn_w, w_in, q_norm_w, k_norm_w, sb_norm_w, conv_w, conv_b, dt_bias, a_log, d_skip, ssd_norm_w, w_out, w_gate_dense, w_up_dense, w_down_dense, w_router, b_router, w_gate_moe, w_up_moe, w_down_moe):
    depth = w_in.shape[0]
    bp, sp, _ = x_prompt.shape
    bs, ss, _ = x_sample.shape
    past = cache_sb_k.shape[2]
    n_moe = w_router.shape[0]

    row = lambda a: a.reshape(depth, 1, a.shape[-1])
    lane_pad = lambda a: jnp.pad(a, ((0, 0), (0, LANES - a.shape[-1]))).reshape(a.shape[0], 1, LANES)
    w_router_pad = jnp.pad(w_router, ((0, 0), (0, 0), (0, LANES - N_EXPERTS)))
    p = {
        'norm_mix_w': row(norm_mix_w), 'norm_ffn_w': row(norm_ffn_w),
        'w_in': jnp.pad(w_in, ((0, 0), (0, 0), (0, IN_MAIN + DT_PAD - w_in.shape[-1]))).astype(BF16),
        'q_norm_w': row(jnp.tile(q_norm_w, (1, N_HEADS))), 'k_norm_w': row(jnp.tile(k_norm_w, (1, N_HEADS))),
        'sb_norm_w': row(sb_norm_w), 'conv_w': conv_w, 'conv_b': row(conv_b),
        'dt_bias': lane_pad(dt_bias), 'a_log': lane_pad(a_log),
        'd_skip': row(jnp.repeat(d_skip, HEAD_DIM, axis=-1)), 'ssd_norm_w': row(ssd_norm_w),
        'w_out': w_out.astype(BF16),
        'w_gate_dense': w_gate_dense.astype(BF16), 'w_up_dense': w_up_dense.astype(BF16),
        'w_down_dense': w_down_dense.astype(BF16),
        'w_router_hi': w_router_pad.astype(BF16),
        'w_router_lo': (w_router_pad - w_router_pad.astype(BF16).astype(F32)).astype(BF16),
        'b_router': jnp.pad(b_router, ((0, 0), (0, LANES - N_EXPERTS)), constant_values=-1e30).reshape(n_moe, 1, LANES),
        'w_gate_moe': w_gate_moe.astype(BF16), 'w_up_moe': w_up_moe.astype(BF16),
        'w_down_moe': w_down_moe.astype(BF16),
    }
    attn_blk = 256
    tm_prompt = 512
    idx = jnp.arange(attn_blk, dtype=jnp.int32)
    wide = jnp.arange(SB_WIDTH, dtype=jnp.int32)
    tok = jnp.arange(tm_prompt, dtype=jnp.int32)
    consts = {
        'from_ones': (idx[:, None] >= idx[None, :]).astype(BF16),
        'before_ones': (tok[None, :] < tok[:, None]).astype(BF16),
        'head_ones': (wide[:, None] // HEAD_DIM == wide[None, :] // HEAD_DIM).astype(BF16),
        'head_expand': (jnp.arange(LANES, dtype=jnp.int32)[:, None] == wide[None, :] // HEAD_DIM).astype(BF16),
    }

    c_all = jnp.concatenate([c_prompt, c_sample], axis=0)
    mod = _modulation(c_all, w_mod, b_mod).reshape(depth, bp + bs, 6, D_MODEL)

    cache_t = lambda a: jnp.transpose(a, (0, 1, 3, 4, 2)).reshape(depth * bs * SB_WIDTH, past)
    kp, vp = cache_t(cache_sb_k), cache_t(cache_sb_v)
    halo_pad = ((0, 0), (0, 0), (CONV_HALO - (CONV_WIDTH - 1), 0), (0, 0))
    conv_in = jnp.pad(state_conv, halo_pad)
    ssd_in = state_ssd.reshape(depth, bs, N_HEADS * HEAD_DIM, D_STATE)
    zero_s = jnp.zeros((bp, N_HEADS * HEAD_DIM, D_STATE), F32)
    zero_c = jnp.zeros((bp, CONV_HALO, CONV_DIM), F32)

    xp, xs = x_prompt, x_sample
    kv_p = kv_s = None
    outs_p, outs_s = [], []
    for l in range(depth):
        xp, k, v, s, c = _layer(xp, mod[l, :bp], l, p, consts, kv_p, None, None, zero_s, zero_c,
                                tm=tm_prompt, ssd_chunk=256, attn_blk=attn_blk, expand_mod=False, sorted_experts=True)
        kv_p = (k, v)
        outs_p.append((s.reshape(bp, N_HEADS, HEAD_DIM, D_STATE), c[:, CONV_HALO - (CONV_WIDTH - 1):]))
        xs, k, v, s, c = _layer(xs, mod[l, bp:], l, p, consts, kv_s, kp, vp, ssd_in[l], conv_in[l],
                                tm=bs * ss, ssd_chunk=ss, attn_blk=attn_blk, expand_mod=True, sorted_experts=False)
        kv_s = (k, v)
        outs_s.append((s.reshape(bs, N_HEADS, HEAD_DIM, D_STATE), c[:, CONV_HALO - (CONV_WIDTH - 1):]))
    stack = lambda outs, i: jnp.stack([o[i] for o in outs])
    heads = lambda a, b, s: a.reshape(depth, b, s, N_HEADS, HEAD_DIM)
    return (xp, xs,
            heads(kv_p[0], bp, sp), heads(kv_p[1], bp, sp), stack(outs_p, 0), stack(outs_p, 1),
            heads(kv_s[0], bs, ss), heads(kv_s[1], bs, ss), stack(outs_s, 0), stack(outs_s, 1))
```

```python
import functools

import jax
import jax.numpy as jnp
from jax import lax
from jax.experimental import pallas as pl
from jax.experimental.pallas import tpu as pltpu

F32 = jnp.float32
BF16 = jnp.bfloat16
HIGHEST = lax.Precision.HIGHEST

EPS = 1e-6
LOG2_E = 1.4426950408889634
D_MODEL = 1024
N_HEADS = 8
HEAD_DIM = 64
SB_WIDTH = 512
SSD_INNER = 512
SSD_GROUPS = 2
D_STATE = 128
CONV_WIDTH = 4
CONV_DIM = 1024
IN_MAIN = 3072
DT_PAD = 128
D_FF = 2816
N_EXPERTS = 8
LANES = 128
CONV_HALO = 8

VMEM_LIMIT = 56 * 1024 * 1024


def _cparams(sem, vmem=VMEM_LIMIT):
    return pltpu.CompilerParams(dimension_semantics=sem, vmem_limit_bytes=vmem)


def _silu(x):
    return x * (1.0 / (1.0 + jnp.exp2(x * -LOG2_E)))


def _softplus(x):
    return jnp.maximum(x, 0.0) + jnp.log(1.0 + jnp.exp(-jnp.abs(x)))


def _rms(x):
    return x * lax.rsqrt(jnp.mean(x * x, axis=-1, keepdims=True) + EPS)


def _mod_kernel(c_ref, w_ref, b_ref, o_ref):
    s = _silu(c_ref[...])
    o_ref[0] = jnp.dot(s, w_ref[0], preferred_element_type=F32, precision=HIGHEST) + b_ref[0]


def _modulation(c_all, w_mod, b_mod):
    depth, _, n6 = w_mod.shape
    ns = c_all.shape[0]
    nj = n6 // D_MODEL
    return pl.pallas_call(
        _mod_kernel,
        grid=(depth, nj),
        in_specs=[pl.BlockSpec((ns, D_MODEL), lambda l, j: (0, 0)),
                  pl.BlockSpec((1, D_MODEL, D_MODEL), lambda l, j: (l, 0, j)),
                  pl.BlockSpec((1, 1, D_MODEL), lambda l, j: (l, 0, j))],
        out_specs=pl.BlockSpec((1, ns, D_MODEL), lambda l, j: (l, 0, j)),
        out_shape=jax.ShapeDtypeStruct((depth, ns, n6), F32),
        compiler_params=_cparams(("arbitrary", "arbitrary")),
        name="adaln_mod",
    )(c_all, w_mod, b_mod.reshape(depth, 1, n6))


def _inproj_kernel(x_ref, sc_ref, sh_ref, nw_ref, w_ref, bd_ref, qw_ref, kw_ref, *rest, nt):
    kf_ref, vf_ref = rest[-5:-3]

    @pl.when(pl.program_id(0) < nt)
    def _():
        _inproj_tile(x_ref, sc_ref, sh_ref, nw_ref, w_ref, bd_ref, qw_ref, kw_ref, *rest[-8:])

    @pl.when(pl.program_id(0) >= nt)
    def _():
        kf_ref[...] = jnp.zeros_like(kf_ref)
        vf_ref[...] = jnp.zeros_like(vf_ref)


def _inproj_tile(x_ref, sc_ref, sh_ref, nw_ref, w_ref, bd_ref, qw_ref, kw_ref,
                 q_ref, k_ref, v_ref, kf_ref, vf_ref, z_ref, xbc_ref, dt_ref):
    h = _rms(x_ref[...]) * nw_ref[...]
    h = h * (1.0 + sc_ref[0]) + sh_ref[0]
    proj = jnp.dot(h.astype(BF16), w_ref[...], preferred_element_type=F32)

    def head_norm(t, w):
        ss = jnp.dot((t * t).astype(BF16), bd_ref[...], preferred_element_type=F32)
        return t * lax.rsqrt(ss * (1.0 / HEAD_DIM) + EPS) * w

    q = head_norm(proj[:, 0:SB_WIDTH], qw_ref[...])
    k = head_norm(proj[:, SB_WIDTH:2 * SB_WIDTH], kw_ref[...])
    v = proj[:, 2 * SB_WIDTH:3 * SB_WIDTH]
    q_ref[...] = (q * (HEAD_DIM ** -0.5 * LOG2_E)).astype(BF16)
    k_ref[...] = k.astype(BF16)
    v_ref[...] = v.astype(BF16)
    for hd in range(N_HEADS):
        cols = slice(hd * HEAD_DIM, (hd + 1) * HEAD_DIM)
        kf_ref[pl.ds(hd, k.shape[0], stride=N_HEADS), :] = k[:, cols]
        vf_ref[pl.ds(hd, v.shape[0], stride=N_HEADS), :] = v[:, cols]
    z_ref[...] = proj[:, 3 * SB_WIDTH:3 * SB_WIDTH + SSD_INNER].astype(BF16)
    xbc_ref[...] = proj[:, 3 * SB_WIDTH + SSD_INNER:IN_MAIN]
    dt_ref[...] = proj[:, IN_MAIN:IN_MAIN + DT_PAD]


def _mod_spec(tm, mr, rows_per_stream, tile=lambda i: i):
    if mr == 1:
        return pl.BlockSpec((1, 1, D_MODEL), lambda i: ((tile(i) * tm) // rows_per_stream, 0, 0))
    return pl.BlockSpec((1, tm, D_MODEL), lambda i: (tile(i), 0, 0))


def _in_proj(x, sc, sh, nw, w_pad, bd, qw, kw, tm, rows_per_stream, layer, depth, kv_states):
    t = x.shape[0]
    mr = sc.shape[1]
    nt = t // tm
    first = kv_states is None
    assert first == (layer == 0)
    tile = lambda i: jnp.minimum(i, nt - 1)
    row = lambda n: pl.BlockSpec((tm, n), lambda i: (tile(i), 0))
    full = lambda a: pl.BlockSpec(a.shape, lambda i: (0,) * a.ndim)
    state = pl.BlockSpec((tm * N_HEADS, HEAD_DIM), lambda i: (layer * nt + i, 0))
    state_shape = jax.ShapeDtypeStruct((depth * t * N_HEADS, HEAD_DIM), F32)
    outs = [(SB_WIDTH, BF16), (SB_WIDTH, BF16), (SB_WIDTH, BF16), None, None,
            (SSD_INNER, BF16), (CONV_DIM, F32), (DT_PAD, F32)]
    in_specs = [row(D_MODEL), _mod_spec(tm, mr, rows_per_stream, tile), _mod_spec(tm, mr, rows_per_stream, tile),
                full(nw), full(w_pad), full(bd), full(qw), full(kw)]
    args = [x, sc, sh, nw, w_pad, bd, qw, kw]
    aliases = {}
    if not first:
        in_specs += [pl.BlockSpec(memory_space=pl.ANY)] * 2
        aliases = {len(args): 3, len(args) + 1: 4}
        args += list(kv_states)
    return pl.pallas_call(
        functools.partial(_inproj_kernel, nt=nt),
        grid=(depth * nt if first else nt,),
        in_specs=in_specs,
        out_specs=[state if o is None else row(o[0]) for o in outs],
        out_shape=[state_shape if o is None else jax.ShapeDtypeStruct((t, o[0]), o[1]) for o in outs],
        input_output_aliases=aliases,
        compiler_params=_cparams(("arbitrary",)),
        name="in_proj",
    )(*args)


def _dot_nt(a, b):
    return lax.dot_general(a, b, (((1,), (1,)), ((), ())), preferred_element_type=F32)


def _sb_accumulate(blocks, from_ones, acc_ref, c_ref):
    acc = acc_ref[...]
    c = c_ref[...]
    for z, weighted, mask in blocks:
        u = jnp.maximum(z, 0.0) + jnp.log(1.0 + jnp.exp2(-jnp.abs(z))) * LOG2_E
        if mask is not None:
            u = jnp.where(mask, u, 0.0)
        cum = jnp.dot(u.astype(BF16), from_ones, preferred_element_type=F32)
        a = jnp.exp2(z - cum)
        if mask is not None:
            a = jnp.where(mask, a, 0.0)
        acc = acc + jnp.exp2(-c) * weighted(a.astype(BF16))
        c = c + cum[:, 0:1]
    acc_ref[...] = acc
    c_ref[...] = c


def _sb_blocks(qs, blocks, from_ones, acc_ref, c_ref):
    _sb_accumulate([(_dot_nt(qs, kblk), functools.partial(jnp.dot, b=vblk, preferred_element_type=F32), mask)
                    for kblk, vblk, mask in blocks], from_ones, acc_ref, c_ref)


def _stack_heads(q):
    lane = lax.broadcasted_iota(jnp.int32, q.shape, 1)
    zero = jnp.zeros_like(q)
    return jnp.concatenate([jnp.where(lane < HEAD_DIM, q, zero), jnp.where(lane >= HEAD_DIM, q, zero)], axis=0)


def _unstack_heads(acc, tq):
    lane = lax.broadcasted_iota(jnp.int32, (tq, LANES), 1)
    return jnp.where(lane < HEAD_DIM, acc[0:tq], acc[tq:2 * tq])


def _strict_lower(n, m):
    r = lax.broadcasted_iota(jnp.int32, (n, m), 0)
    c = lax.broadcasted_iota(jnp.int32, (n, m), 1)
    return c < r


def _stacked_causal(tq, tk):
    r = lax.broadcasted_iota(jnp.int32, (2 * tq, tk), 0)
    c = lax.broadcasted_iota(jnp.int32, (2 * tq, tk), 1)
    return c < jnp.where(r >= tq, r - tq, r)


GROUP = 8


def _attn_prompt_kernel(q_ref, k_ref, v_ref, m_ref, o_ref, acc_ref, c_ref, *, blk):
    i = pl.program_id(2)
    qs = _stack_heads(q_ref[0])
    acc_ref[...] = jnp.zeros_like(acc_ref)
    c_ref[...] = jnp.zeros_like(c_ref)
    from_ones = m_ref[...]
    diag_mask = _stacked_causal(blk, blk)

    def kv(kb, mask=None):
        start = pl.multiple_of(kb * blk, blk)
        return k_ref[0, pl.ds(start, blk), :], v_ref[0, pl.ds(start, blk), :], mask

    def run(top, n, first_mask=None):
        blocks = [kv(top - d, first_mask if d == 0 else None) for d in range(n)]
        _sb_blocks(qs, blocks, from_ones, acc_ref, c_ref)

    lead = jnp.minimum(i + 1, GROUP)
    for n in range(1, GROUP + 1):
        @pl.when(lead == n)
        def _(n=n):
            run(i, n, diag_mask)

    rest = i + 1 - lead
    full = rest // GROUP

    def body(j, carry):
        run(rest - 1 - GROUP * j, GROUP)
        return carry

    lax.fori_loop(0, full, body, 0)
    tail = rest - GROUP * full
    for n in range(1, GROUP):
        @pl.when(tail == n)
        def _(n=n):
            run(n - 1, n)

    o_ref[0] = _unstack_heads(acc_ref[...], blk).astype(o_ref.dtype)


def _attn_prompt(q, k, v, from_ones, blk):
    b, s, _ = q.shape
    npair = SB_WIDTH // LANES
    qspec = pl.BlockSpec((1, blk, LANES), lambda bi, p, i: (bi, i, p))
    kvspec = pl.BlockSpec((1, s, LANES), lambda bi, p, i: (bi, 0, p))
    return pl.pallas_call(
        functools.partial(_attn_prompt_kernel, blk=blk),
        grid=(b, npair, s // blk),
        in_specs=[qspec, kvspec, kvspec, pl.BlockSpec(from_ones.shape, lambda bi, p, i: (0, 0))],
        out_specs=qspec,
        out_shape=jax.ShapeDtypeStruct((b, s, SB_WIDTH), BF16),
        scratch_shapes=[pltpu.VMEM((2 * blk, LANES), F32), pltpu.VMEM((2 * blk, 1), F32)],
        compiler_params=_cparams(("arbitrary", "arbitrary", "arbitrary")),
        name="sb_attn_prompt",
    )(q, k, v, from_ones)


PAST_BLOCKS_PER_STEP = 2


def _attn_sample_kernel(q_ref, kn_ref, vn_ref, kp_ref, vp_ref, m_ref, o_ref, qbd_ref, acc_ref, c_ref, *, blk):
    j = pl.program_id(1)
    tq = q_ref.shape[1]
    rows = N_HEADS * tq
    from_ones = m_ref[...]
    row_head = lax.broadcasted_iota(jnp.int32, (rows, SB_WIDTH), 0) // tq
    lane_head = lax.broadcasted_iota(jnp.int32, (rows, SB_WIDTH), 1) // HEAD_DIM

    @pl.when(j == 0)
    def _():
        q = q_ref[0]
        stacked = jnp.concatenate([q] * N_HEADS, axis=0)
        qbd_ref[...] = jnp.where(row_head == lane_head, stacked, jnp.zeros_like(stacked))
        acc_ref[...] = jnp.zeros_like(acc_ref)
        c_ref[...] = jnp.zeros_like(c_ref)
        r = lax.broadcasted_iota(jnp.int32, (rows, blk), 0)
        mask = lax.broadcasted_iota(jnp.int32, (rows, blk), 1) < r - (r // tq) * tq
        block = (_dot_nt(qbd_ref[...], kn_ref[0]), functools.partial(jnp.dot, b=vn_ref[0], preferred_element_type=F32),
                 mask)
        _sb_accumulate([block], from_ones, acc_ref, c_ref)

    @pl.when(j > 0)
    def _():
        blocks = []
        for sub in reversed(range(PAST_BLOCKS_PER_STEP)):
            cols = slice(sub * blk, (sub + 1) * blk)
            z = jnp.dot(qbd_ref[...], kp_ref[:, cols].astype(BF16), preferred_element_type=F32)
            blocks.append((z, functools.partial(_dot_nt, b=vp_ref[:, cols].astype(BF16)), None))
        _sb_accumulate(blocks, from_ones, acc_ref, c_ref)

    @pl.when(j == pl.num_programs(1) - 1)
    def _():
        kept = jnp.where(row_head == lane_head, acc_ref[...], 0.0)
        out = kept[0:tq]
        for hd in range(1, N_HEADS):
            out = out + kept[hd * tq:(hd + 1) * tq]
        o_ref[0] = out.astype(o_ref.dtype)


def _attn_sample(q, k_new, v_new, k_past_t, v_past_t, layer, from_ones, blk):
    b, tq, _ = q.shape
    past = k_past_t.shape[1]
    span = PAST_BLOCKS_PER_STEP * blk
    nsteps = past // span
    assert nsteps * span == past
    nspec = pl.BlockSpec((1, tq, SB_WIDTH), lambda bi, j: (bi, 0, 0))
    kspec = pl.BlockSpec((1, blk, SB_WIDTH), lambda bi, j: (bi, 0, 0))
    pspec = pl.BlockSpec((SB_WIDTH, span), lambda bi, j: (layer * b + bi, nsteps - jnp.maximum(j, 1)))
    pad = ((0, 0), (0, blk - tq), (0, 0))
    k_new, v_new = jnp.pad(k_new, pad), jnp.pad(v_new, pad)
    rows = N_HEADS * tq
    return pl.pallas_call(
        functools.partial(_attn_sample_kernel, blk=blk),
        grid=(b, 1 + nsteps),
        in_specs=[nspec, kspec, kspec, pspec, pspec, pl.BlockSpec(from_ones.shape, lambda bi, j: (0, 0))],
        out_specs=nspec,
        out_shape=jax.ShapeDtypeStruct((b, tq, SB_WIDTH), BF16),
        scratch_shapes=[pltpu.VMEM((rows, SB_WIDTH), BF16), pltpu.VMEM((rows, SB_WIDTH), F32),
                        pltpu.VMEM((rows, 1), F32)],
        compiler_params=_cparams(("arbitrary", "arbitrary")),
        name="sb_attn_sample",
    )(q, k_new, v_new, k_past_t, v_past_t, from_ones)


def _ssd_kernel(xbc_ref, z_ref, dt_ref, cw_ref, cb_ref, dtb_ref, alog_ref, dsk_ref, nw_ref, exp_ref,
                s0_ref, c0_ref, y_ref, sT_out_ref, cs_out_ref, ext_ref, st_ref, *, L):
    ci = pl.program_id(1)
    nc = pl.num_programs(1)
    gw = SSD_INNER // SSD_GROUPS
    hpg = N_HEADS // SSD_GROUPS

    @pl.when(ci == 0)
    def _():
        ext_ref[0:CONV_HALO, :] = c0_ref[0]
        st_ref[...] = s0_ref[0].T

    ext_ref[CONV_HALO:CONV_HALO + L, :] = xbc_ref[0]
    conv = cb_ref[...]
    for w in range(CONV_WIDTH):
        off = CONV_HALO - (CONV_WIDTH - 1) + w
        conv = conv + ext_ref[off:off + L, :] * cw_ref[w:w + 1, :]
    xbc = _silu(conv)
    tail = ext_ref[L:L + CONV_HALO, :]
    ext_ref[0:CONV_HALO, :] = tail

    x = xbc[:, 0:SSD_INNER]
    bm = xbc[:, SSD_INNER:SSD_INNER + SSD_GROUPS * D_STATE]
    cm = xbc[:, SSD_INNER + SSD_GROUPS * D_STATE:]
    dt = _softplus(dt_ref[0] + dtb_ref[...])
    da = dt * (-jnp.exp(alog_ref[...]))
    tril = jnp.where(_strict_lower(L, L) | (lax.broadcasted_iota(jnp.int32, (L, L), 0)
                                            == lax.broadcasted_iota(jnp.int32, (L, L), 1)), 1.0, 0.0)
    a_cum = jnp.dot(tril, da * LOG2_E, preferred_element_type=F32, precision=HIGHEST)
    a_cum_t = a_cum.T
    a_last = a_cum[L - 1:L, :]
    expand = exp_ref[...]

    def bcast(t):
        hi = t.astype(BF16)
        lo = (t - hi.astype(F32)).astype(BF16)
        return (jnp.dot(hi, expand, preferred_element_type=F32) + jnp.dot(lo, expand, preferred_element_type=F32))

    per_head = bcast(jnp.concatenate([dt, jnp.exp2(a_cum), jnp.exp2(a_last - a_cum)], axis=0))
    dt_e, ea_e, ds_e = per_head[0:L], per_head[L:2 * L], per_head[2 * L:3 * L]
    xdt = x * dt_e
    wgt = (xdt * ds_e).astype(BF16)
    xdt_b = xdt.astype(BF16)
    bm_t = bm.T.astype(BF16)
    bm_b = bm.astype(BF16)
    cm_b = cm.astype(BF16)
    causal = _strict_lower(L, L) | (lax.broadcasted_iota(jnp.int32, (L, L), 0)
                                    == lax.broadcasted_iota(jnp.int32, (L, L), 1))
    lane_g = lax.broadcasted_iota(jnp.int32, (L, gw), 1)
    y_parts = []
    for g in range(SSD_GROUPS):
        gs = slice(g * gw, (g + 1) * gw)
        ns = slice(g * D_STATE, (g + 1) * D_STATE)
        cb = lax.dot_general(cm_b[:, ns], bm_b[:, ns], (((1,), (1,)), ((), ())), preferred_element_type=F32)
        st_g = st_ref[:, gs]
        y_g = jnp.dot(cm_b[:, ns], st_g.astype(BF16), preferred_element_type=F32) * ea_e[:, gs]
        for hl in range(hpg):
            hd = g * hpg + hl
            seg = a_cum[:, hd:hd + 1] - a_cum_t[hd:hd + 1, :]
            decay = jnp.where(causal, jnp.exp2(seg), 0.0)
            scores = (cb * decay).astype(BF16)
            xh = jnp.where((lane_g >= hl * HEAD_DIM) & (lane_g < (hl + 1) * HEAD_DIM), xdt_b[:, gs],
                           jnp.zeros_like(xdt_b[:, gs]))
            y_g = y_g + jnp.dot(scores, xh, preferred_element_type=F32)
        y_parts.append(y_g)
        new_states = jnp.dot(bm_t[ns, :], wgt[:, gs], preferred_element_type=F32)
        st_ref[:, gs] = st_g * ea_e[L - 1:L, gs] + new_states
    y = jnp.concatenate(y_parts, axis=-1) + dsk_ref[...] * x
    y = y * _silu(z_ref[0].astype(F32))
    y = jnp.concatenate([_rms(y[:, g * gw:(g + 1) * gw]) for g in range(SSD_GROUPS)], axis=-1) * nw_ref[...]
    y_ref[0] = y.astype(y_ref.dtype)

    @pl.when(ci == nc - 1)
    def _():
        sT_out_ref[0] = st_ref[...].T
        cs_out_ref[0] = tail


def _ssd(xbc, z, dt, cw, cb, dtb, alog, dsk_e, nw, expand, s0, c0, L):
    b, s, _ = xbc.shape
    seq = lambda n: pl.BlockSpec((1, L, n), lambda bi, ci: (bi, ci, 0))
    full = lambda a: pl.BlockSpec(a.shape, lambda bi, ci: (0,) * a.ndim)
    per_b = lambda a: pl.BlockSpec((1,) + a.shape[1:], lambda bi, ci: (bi,) + (0,) * (a.ndim - 1))
    return pl.pallas_call(
        functools.partial(_ssd_kernel, L=L),
        grid=(b, s // L),
        in_specs=[seq(CONV_DIM), seq(SSD_INNER), seq(DT_PAD), full(cw), full(cb), full(dtb), full(alog),
                  full(dsk_e), full(nw), full(expand), per_b(s0), per_b(c0)],
        out_specs=[seq(SSD_INNER), per_b(s0), per_b(c0)],
        out_shape=[jax.ShapeDtypeStruct((b, s, SSD_INNER), BF16),
                   jax.ShapeDtypeStruct(s0.shape, F32), jax.ShapeDtypeStruct(c0.shape, F32)],
        scratch_shapes=[pltpu.VMEM((L + CONV_HALO, CONV_DIM), F32), pltpu.VMEM((D_STATE, SSD_INNER), F32)],
        compiler_params=_cparams(("arbitrary", "arbitrary")),
        name="ssd_mixer",
    )(xbc, z, dt, cw, cb, dtb, alog, dsk_e, nw, expand, s0, c0)


def _mix_residual_norm(sb_ref, y_ref, x_ref, sbw_ref, w_ref, g_ref, nw_ref, sc_ref, sh_ref):
    sbn = _rms(sb_ref[...].astype(F32)) * sbw_ref[...]
    cat = jnp.concatenate([sbn.astype(BF16), y_ref[...]], axis=-1)
    mix = jnp.dot(cat, w_ref[...], preferred_element_type=F32)
    x1 = x_ref[...] + g_ref[0] * mix
    h2 = _rms(x1) * nw_ref[...]
    return x1, h2 * (1.0 + sc_ref[0]) + sh_ref[0]


def _outproj_kernel(sb_ref, y_ref, x_ref, sbw_ref, w_ref, g_ref, nw_ref, sc_ref, sh_ref, x1_ref, h2_ref):
    x1, h2 = _mix_residual_norm(sb_ref, y_ref, x_ref, sbw_ref, w_ref, g_ref, nw_ref, sc_ref, sh_ref)
    x1_ref[...] = x1
    h2_ref[...] = h2.astype(h2_ref.dtype)


R_E1, R_E2, R_W1, R_W2, R_RANK1, R_RANK2 = range(6)


def _outproj_router_kernel(sb_ref, y_ref, x_ref, sbw_ref, w_ref, g_ref, nw_ref, sc_ref, sh_ref,
                           wrh_ref, wrl_ref, br_ref, tri_ref,
                           x1_ref, h2_ref, gates_ref, route_ref, count_ref):
    x1, h2 = _mix_residual_norm(sb_ref, y_ref, x_ref, sbw_ref, w_ref, g_ref, nw_ref, sc_ref, sh_ref)
    x1_ref[...] = x1
    h2_ref[...] = h2.astype(h2_ref.dtype)

    @pl.when(pl.program_id(0) == 0)
    def _():
        count_ref[...] = jnp.zeros_like(count_ref)

    h_hi = h2.astype(BF16)
    h_lo = (h2 - h_hi.astype(F32)).astype(BF16)
    dot = lambda a, b: jnp.dot(a, b, preferred_element_type=F32)
    logits = dot(h_hi, wrh_ref[...]) + (dot(h_lo, wrh_ref[...]) + dot(h_hi, wrl_ref[...])) + br_ref[...]
    lane = lax.broadcasted_iota(jnp.int32, logits.shape, 1)
    m1 = jnp.max(logits, axis=-1, keepdims=True)
    i1 = jnp.min(jnp.where(logits == m1, lane, LANES), axis=-1, keepdims=True)
    rest = jnp.where(lane == i1, -jnp.inf, logits)
    m2 = jnp.max(rest, axis=-1, keepdims=True)
    i2 = jnp.min(jnp.where(rest == m2, lane, LANES), axis=-1, keepdims=True)
    w1 = 1.0 / (1.0 + jnp.exp(m2 - m1))
    w2 = 1.0 - w1
    hot1 = lane == i1
    hot2 = lane == i2
    gates_ref[...] = jnp.where(hot1, w1, 0.0) + jnp.where(hot2, w2, 0.0)
    hot = jnp.where(hot1 | hot2, 1.0, 0.0)
    before = dot(tri_ref[...], hot.astype(BF16)) + count_ref[...]
    rank1 = jnp.sum(jnp.where(hot1, before, 0.0), axis=-1, keepdims=True)
    rank2 = jnp.sum(jnp.where(hot2, before, 0.0), axis=-1, keepdims=True)
    count_ref[...] += jnp.sum(hot, axis=0, keepdims=True)
    rec = jnp.zeros(logits.shape, F32)
    for ln, val in ((R_E1, i1.astype(F32)), (R_E2, i2.astype(F32)), (R_W1, w1), (R_W2, w2),
                    (R_RANK1, rank1), (R_RANK2, rank2)):
        rec = jnp.where(lane == ln, val, rec)
    route_ref[...] = rec


def _out_proj(sb, y, x, sbw, w_out, g, nw, sc, sh, tm, rows_per_stream, h2_dtype, router=None):
    t = x.shape[0]
    mr = g.shape[1]
    row = lambda n: pl.BlockSpec((tm, n), lambda i: (i, 0))
    full = lambda a: pl.BlockSpec(a.shape, lambda i: (0,) * a.ndim)
    ms = _mod_spec(tm, mr, rows_per_stream)
    in_specs = [row(SB_WIDTH), row(SSD_INNER), row(D_MODEL), full(sbw), full(w_out), ms, full(nw), ms, ms]
    out_specs = [row(D_MODEL), row(D_MODEL)]
    out_shape = [jax.ShapeDtypeStruct((t, D_MODEL), F32), jax.ShapeDtypeStruct((t, D_MODEL), h2_dtype)]
    args = [sb, y, x, sbw, w_out, g, nw, sc, sh]
    body = _outproj_kernel
    if router is not None:
        body = _outproj_router_kernel
        args += list(router)
        in_specs += [full(a) for a in router]
        out_specs += [row(LANES), row(LANES), pl.BlockSpec((1, LANES), lambda i: (0, 0))]
        out_shape += [jax.ShapeDtypeStruct((t, LANES), F32), jax.ShapeDtypeStruct((t, LANES), F32),
                      jax.ShapeDtypeStruct((1, LANES), F32)]
    return pl.pallas_call(
        body,
        grid=(t // tm,),
        in_specs=in_specs,
        out_specs=out_specs,
        out_shape=out_shape,
        compiler_params=_cparams(("arbitrary",)),
        name="out_proj" if router is None else "out_proj_router",
    )(*args)


def _ffn_kernel(h_ref, wg_ref, wu_ref, wd_ref, x_ref, g_ref, *rest, n_exp):
    gates_ref = rest[0] if n_exp > 1 else None
    o_ref, acc_ref = rest[-2:]
    e = pl.program_id(1)
    j = pl.program_id(2)

    @pl.when((e == 0) & (j == 0))
    def _():
        acc_ref[...] = jnp.zeros_like(acc_ref)

    part = _swiglu_partial(h_ref[...], wg_ref, wu_ref, wd_ref)
    if n_exp > 1:
        lane = lax.broadcasted_iota(jnp.int32, gates_ref.shape, 1)
        part = part * jnp.sum(jnp.where(lane == e, gates_ref[...], 0.0), axis=-1, keepdims=True)
    acc_ref[...] += part

    @pl.when((e == n_exp - 1) & (j == pl.num_programs(2) - 1))
    def _():
        o_ref[...] = x_ref[...] + g_ref[0] * acc_ref[...]


def _ffn(h, wg, wu, wd, x, g, gates, tm, tf, rows_per_stream):
    t = x.shape[0]
    n_exp = wg.shape[0]
    mr = g.shape[1]
    if mr == 1:
        gspec = pl.BlockSpec((1, 1, D_MODEL), lambda i, e, j: ((i * tm) // rows_per_stream, 0, 0))
    else:
        gspec = pl.BlockSpec((1, tm, D_MODEL), lambda i, e, j: (i, 0, 0))
    row = lambda n: pl.BlockSpec((tm, n), lambda i, e, j: (i, 0))
    in_specs = [row(D_MODEL),
                pl.BlockSpec((1, D_MODEL, tf), lambda i, e, j: (e, 0, j)),
                pl.BlockSpec((1, D_MODEL, tf), lambda i, e, j: (e, 0, j)),
                pl.BlockSpec((1, tf, D_MODEL), lambda i, e, j: (e, j, 0)),
                row(D_MODEL), gspec]
    args = [h, wg, wu, wd, x, g]
    if n_exp > 1:
        in_specs.append(row(LANES))
        args.append(gates)
    return pl.pallas_call(
        functools.partial(_ffn_kernel, n_exp=n_exp),
        grid=(t // tm, n_exp, D_FF // tf),
        in_specs=in_specs,
        out_specs=row(D_MODEL),
        out_shape=jax.ShapeDtypeStruct((t, D_MODEL), F32),
        scratch_shapes=[pltpu.VMEM((tm, D_MODEL), F32)],
        compiler_params=_cparams(("arbitrary", "arbitrary", "arbitrary")),
        name="ffn_dense" if n_exp == 1 else "ffn_experts",
    )(*args)


EXPERT_TILE = 512


def _dispatch_kernel(pos_ref, h_ref, zeros_hbm, xs_hbm, sem, *, tm):
    del zeros_hbm

    def copy(r, slot):
        return pltpu.make_async_copy(h_ref.at[pl.ds(r, 1)], xs_hbm.at[pl.ds(pos_ref[0, 0, slot * tm + r], 1)], sem)

    def issue(r, carry):
        copy(r, 0).start(priority=0)
        copy(r, 1).start(priority=1)
        return carry

    def drain(r, carry):
        copy(r, 0).wait()
        copy(r, 1).wait()
        return carry

    lax.fori_loop(0, tm, issue, 0, unroll=8)
    lax.fori_loop(0, tm, drain, 0, unroll=8)


def _dispatch(h, pos, n_rows, tm):
    t = h.shape[0]
    return pl.pallas_call(
        functools.partial(_dispatch_kernel, tm=tm),
        grid=(t // tm,),
        in_specs=[pl.BlockSpec((1, 1, 2 * tm), lambda i: (i, 0, 0), memory_space=pltpu.SMEM),
                  pl.BlockSpec((tm, D_MODEL), lambda i: (i, 0)), pl.BlockSpec(memory_space=pl.ANY)],
        out_specs=pl.BlockSpec(memory_space=pl.ANY),
        out_shape=jax.ShapeDtypeStruct((n_rows, D_MODEL), F32),
        scratch_shapes=[pltpu.SemaphoreType.DMA(())],
        input_output_aliases={2: 0},
        compiler_params=_cparams(("arbitrary",)),
        name="moe_dispatch",
    )(pos, h, jnp.zeros((n_rows, D_MODEL), F32))


def _swiglu_partial(h, wg_ref, wu_ref, wd_ref):
    gt = jnp.dot(h, wg_ref[0], preferred_element_type=F32)
    up = jnp.dot(h, wu_ref[0], preferred_element_type=F32)
    act = (_silu(gt) * up).astype(BF16)
    return jnp.dot(act, wd_ref[0], preferred_element_type=F32)


def _grouped_ffn_kernel(tile_expert_ref, n_active_ref, x_ref, wg_ref, wu_ref, wd_ref, o_ref):
    del tile_expert_ref
    j = pl.program_id(1)
    active = pl.program_id(0) < n_active_ref[0]

    @pl.when(active & (j == 0))
    def _():
        o_ref[...] = _swiglu_partial(x_ref[...].astype(BF16), wg_ref, wu_ref, wd_ref)

    @pl.when(active & (j > 0))
    def _():
        o_ref[...] += _swiglu_partial(x_ref[...].astype(BF16), wg_ref, wu_ref, wd_ref)

    @pl.when(jnp.logical_not(active) & (j == 0))
    def _():
        o_ref[...] = jnp.zeros_like(o_ref)


def _grouped_ffn(xs, tile_expert, n_active, wg, wu, wd, tf):
    n_rows = xs.shape[0]
    row = pl.BlockSpec((EXPERT_TILE, D_MODEL), lambda i, j, te, na: (i, 0))
    grid_spec = pltpu.PrefetchScalarGridSpec(
        num_scalar_prefetch=2,
        grid=(n_rows // EXPERT_TILE, D_FF // tf),
        in_specs=[row,
                  pl.BlockSpec((1, D_MODEL, tf), lambda i, j, te, na: (te[i], 0, j)),
                  pl.BlockSpec((1, D_MODEL, tf), lambda i, j, te, na: (te[i], 0, j)),
                  pl.BlockSpec((1, tf, D_MODEL), lambda i, j, te, na: (te[i], j, 0))],
        out_specs=row)
    return pl.pallas_call(
        _grouped_ffn_kernel,
        grid_spec=grid_spec,
        out_shape=jax.ShapeDtypeStruct((n_rows, D_MODEL), F32),
        compiler_params=_cparams(("arbitrary", "arbitrary")),
        name="ffn_grouped",
    )(tile_expert, n_active, xs, wg, wu, wd)


def _combine_kernel(pos_ref, nxt_ref, route_ref, x_ref, g_ref, ye_hbm, o_ref, buf_ref, sem_ref, *, tm):
    i = pl.program_id(0)
    n = pl.num_programs(0)
    slot = i % 2

    def copy(idx_ref, r, which, s):
        return pltpu.make_async_copy(ye_hbm.at[pl.ds(idx_ref[0, 0, which * tm + r], 1)],
                                     buf_ref.at[s, pl.ds(which * tm + r, 1)], sem_ref.at[s])

    def fetch(idx_ref, s):
        def issue(r, carry):
            copy(idx_ref, r, 0, s).start(priority=0)
            copy(idx_ref, r, 1, s).start(priority=1)
            return carry
        lax.fori_loop(0, tm, issue, 0, unroll=8)

    @pl.when(i == 0)
    def _():
        fetch(pos_ref, 0)

    @pl.when(i + 1 < n)
    def _():
        fetch(nxt_ref, 1 - slot)

    def drain(r, carry):
        copy(pos_ref, r, 0, slot).wait()
        copy(pos_ref, r, 1, slot).wait()
        return carry

    lax.fori_loop(0, tm, drain, 0, unroll=8)
    rec = route_ref[...]
    w1 = rec[:, R_W1:R_W1 + 1]
    w2 = rec[:, R_W2:R_W2 + 1]
    rows = buf_ref[slot]
    o_ref[...] = x_ref[...] + g_ref[0] * (w1 * rows[0:tm] + w2 * rows[tm:2 * tm])


def _combine(ye, pos, route, x, g, tm, rows_per_stream):
    t = x.shape[0]
    n = t // tm
    row = lambda w: pl.BlockSpec((tm, w), lambda i: (i, 0))
    smem = lambda f: pl.BlockSpec((1, 1, 2 * tm), f, memory_space=pltpu.SMEM)
    return pl.pallas_call(
        functools.partial(_combine_kernel, tm=tm),
        grid=(n,),
        in_specs=[smem(lambda i: (i, 0, 0)), smem(lambda i: (jnp.minimum(i + 1, n - 1), 0, 0)),
                  row(LANES), row(D_MODEL), _mod_spec(tm, g.shape[1], rows_per_stream),
                  pl.BlockSpec(memory_space=pl.ANY)],
        out_specs=row(D_MODEL),
        out_shape=jax.ShapeDtypeStruct((t, D_MODEL), F32),
        scratch_shapes=[pltpu.VMEM((2, 2 * tm, D_MODEL), F32), pltpu.SemaphoreType.DMA((2,))],
        compiler_params=_cparams(("arbitrary",)),
        name="moe_combine",
    )(pos, pos, route, x, g, ye)


def _sorted_experts(h2f, route, counts, x1, g_f, wg, wu, wd, tm, rows_per_stream):
    t = h2f.shape[0]
    n_tiles = (2 * t) // EXPERT_TILE + N_EXPERTS
    cnt = counts[0, :N_EXPERTS].astype(jnp.int32)
    padded = ((cnt + EXPERT_TILE - 1) // EXPERT_TILE) * EXPERT_TILE
    ends = jnp.cumsum(padded)
    starts = ends - padded
    experts = jnp.arange(N_EXPERTS, dtype=jnp.int32)

    def slots(e_lane, r_lane):
        e = route[:, e_lane].astype(jnp.int32)
        start = jnp.sum(jnp.where(e[:, None] == experts[None, :], starts[None, :], 0), axis=-1)
        return start + route[:, r_lane].astype(jnp.int32)

    pos = jnp.stack([slots(R_E1, R_RANK1).reshape(t // tm, tm), slots(R_E2, R_RANK2).reshape(t // tm, tm)], axis=1)
    pos = pos.reshape(t // tm, 1, 2 * tm)
    tile_start = jnp.arange(n_tiles, dtype=jnp.int32) * EXPERT_TILE
    n_active = (ends[-1] // EXPERT_TILE).astype(jnp.int32)
    clipped = jnp.minimum(tile_start, ends[-1] - EXPERT_TILE)
    tile_expert = jnp.sum((clipped[:, None] >= ends[None, :]).astype(jnp.int32), axis=-1)
    xs = _dispatch(h2f, pos, n_tiles * EXPERT_TILE, tm)
    ye = _grouped_ffn(xs, tile_expert, n_active.reshape(1), wg, wu, wd, D_FF // 2)
    return _combine(ye, pos, route, x1, g_f, tm, rows_per_stream)


def _layer(x, mod, l, p, consts, kv_states, k_past, v_past, s0, c0, tm, ssd_chunk, attn_blk, expand_mod,
           sorted_experts):
    b, s, _ = x.shape
    t = b * s

    def mod_rows(i):
        m = mod[:, i, :]
        if expand_mod:
            return jnp.repeat(m, s, axis=0).reshape(t // tm, tm, D_MODEL)
        return m.reshape(b, 1, D_MODEL)

    sh_m, sc_m, g_m, sh_f, sc_f, g_f = [mod_rows(i) for i in range(6)]
    xf = x.reshape(t, D_MODEL)
    q, k, v, kf, vf, z, xbc, dt = _in_proj(xf, sc_m, sh_m, p['norm_mix_w'][l], p['w_in'][l], consts['head_ones'],
                                           p['q_norm_w'][l], p['k_norm_w'][l], tm, s, l, p['w_in'].shape[0], kv_states)
    shp = lambda a: a.reshape(b, s, a.shape[-1])
    if k_past is None:
        sb = _attn_prompt(shp(q), shp(k), shp(v), consts['from_ones'], attn_blk)
    else:
        sb = _attn_sample(shp(q), shp(k), shp(v), k_past, v_past, l, consts['from_ones'], attn_blk)
    y, s_new, c_new = _ssd(shp(xbc), shp(z), shp(dt), p['conv_w'][l], p['conv_b'][l], p['dt_bias'][l],
                           p['a_log'][l], p['d_skip'][l], p['ssd_norm_w'][l], consts['head_expand'],
                           s0, c0, ssd_chunk)
    mix_args = (sb.reshape(t, SB_WIDTH), y.reshape(t, SSD_INNER), xf, p['sb_norm_w'][l], p['w_out'][l], g_m,
                p['norm_ffn_w'][l], sc_f, sh_f, tm, s)
    i = l // 2
    tf = D_FF // 2
    if l % 2 == 0:
        x1, h2 = _out_proj(*mix_args, BF16)
        x2 = _ffn(h2, p['w_gate_dense'][i:i + 1], p['w_up_dense'][i:i + 1], p['w_down_dense'][i:i + 1],
                  x1, g_f, None, tm, tf, s)
    else:
        router = (p['w_router_hi'][i], p['w_router_lo'][i], p['b_router'][i], consts['before_ones'][:tm, :tm])
        experts = (p['w_gate_moe'][i], p['w_up_moe'][i], p['w_down_moe'][i])
        if sorted_experts:
            x1, h2, _, route, counts = _out_proj(*mix_args, F32, router)
            x2 = _sorted_experts(h2, route, counts, x1, g_f, *experts, tm, s)
        else:
            x1, h2, gates, _, _ = _out_proj(*mix_args, BF16, router)
            x2 = _ffn(h2, *experts, x1, g_f, gates, tm, tf, s)
    return x2.reshape(b, s, D_MODEL), kf, vf, s_new, c_new


def kernel(x_prompt, x_sample, c_prompt, c_sample, cache_sb_k, cache_sb_v, state_ssd, state_conv, w_mod, b_mod, norm_mix_w, norm_ffn_w, w_in, q_norm_w, k_norm_w, sb_norm_w, conv_w, conv_b, dt_bias, a_log, d_skip, ssd_norm_w, w_out, w_gate_dense, w_up_dense, w_down_dense, w_router, b_router, w_gate_moe, w_up_moe, w_down_moe):
    depth = w_in.shape[0]
    bp, sp, _ = x_prompt.shape
    bs, ss, _ = x_sample.shape
    past = cache_sb_k.shape[2]
    n_moe = w_router.shape[0]

    row = lambda a: a.reshape(depth, 1, a.shape[-1])
    lane_pad = lambda a: jnp.pad(a, ((0, 0), (0, LANES - a.shape[-1]))).reshape(a.shape[0], 1, LANES)
    w_router_pad = jnp.pad(w_router, ((0, 0), (0, 0), (0, LANES - N_EXPERTS)))
    p = {
        'norm_mix_w': row(norm_mix_w), 'norm_ffn_w': row(norm_ffn_w),
        'w_in': jnp.pad(w_in, ((0, 0), (0, 0), (0, IN_MAIN + DT_PAD - w_in.shape[-1]))).astype(BF16),
        'q_norm_w': row(jnp.tile(q_norm_w, (1, N_HEADS))), 'k_norm_w': row(jnp.tile(k_norm_w, (1, N_HEADS))),
        'sb_norm_w': row(sb_norm_w), 'conv_w': conv_w, 'conv_b': row(conv_b),
        'dt_bias': lane_pad(dt_bias), 'a_log': lane_pad(a_log),
        'd_skip': row(jnp.repeat(d_skip, HEAD_DIM, axis=-1)), 'ssd_norm_w': row(ssd_norm_w),
        'w_out': w_out.astype(BF16),
        'w_gate_dense': w_gate_dense.astype(BF16), 'w_up_dense': w_up_dense.astype(BF16),
        'w_down_dense': w_down_dense.astype(BF16),
        'w_router_hi': w_router_pad.astype(BF16),
        'w_router_lo': (w_router_pad - w_router_pad.astype(BF16).astype(F32)).astype(BF16),
        'b_router': jnp.pad(b_router, ((0, 0), (0, LANES - N_EXPERTS)), constant_values=-1e30).reshape(n_moe, 1, LANES),
        'w_gate_moe': w_gate_moe.astype(BF16), 'w_up_moe': w_up_moe.astype(BF16),
        'w_down_moe': w_down_moe.astype(BF16),
    }
    attn_blk = 256
    tm_prompt = 512
    idx = jnp.arange(attn_blk, dtype=jnp.int32)
    wide = jnp.arange(SB_WIDTH, dtype=jnp.int32)
    tok = jnp.arange(tm_prompt, dtype=jnp.int32)
    consts = {
        'from_ones': (idx[:, None] >= idx[None, :]).astype(BF16),
        'before_ones': (tok[None, :] < tok[:, None]).astype(BF16),
        'head_ones': (wide[:, None] // HEAD_DIM == wide[None, :] // HEAD_DIM).astype(BF16),
        'head_expand': (jnp.arange(LANES, dtype=jnp.int32)[:, None] == wide[None, :] // HEAD_DIM).astype(BF16),
    }

    c_all = jnp.concatenate([c_prompt, c_sample], axis=0)
    mod = _modulation(c_all, w_mod, b_mod).reshape(depth, bp + bs, 6, D_MODEL)

    cache_t = lambda a: jnp.transpose(a, (0, 1, 3, 4, 2)).reshape(depth * bs * SB_WIDTH, past)
    kp, vp = cache_t(cache_sb_k), cache_t(cache_sb_v)
    halo_pad = ((0, 0), (0, 0), (CONV_HALO - (CONV_WIDTH - 1), 0), (0, 0))
    conv_in = jnp.pad(state_conv, halo_pad)
    ssd_in = state_ssd.reshape(depth, bs, N_HEADS * HEAD_DIM, D_STATE)
    zero_s = jnp.zeros((bp, N_HEADS * HEAD_DIM, D_STATE), F32)
    zero_c = jnp.zeros((bp, CONV_HALO, CONV_DIM), F32)

    xp, xs = x_prompt, x_sample
    kv_p = kv_s = None
    outs_p, outs_s = [], []
    for l in range(depth):
        xp, k, v, s, c = _layer(xp, mod[l, :bp], l, p, consts, kv_p, None, None, zero_s, zero_c,
                                tm=tm_prompt, ssd_chunk=256, attn_blk=attn_blk, expand_mod=False, sorted_experts=True)
        kv_p = (k, v)
        outs_p.append((s.reshape(bp, N_HEADS, HEAD_DIM, D_STATE), c[:, CONV_HALO - (CONV_WIDTH - 1):]))
        xs, k, v, s, c = _layer(xs, mod[l, bp:], l, p, consts, kv_s, kp, vp, ssd_in[l], conv_in[l],
                                tm=bs * ss, ssd_chunk=ss, attn_blk=attn_blk, expand_mod=True, sorted_experts=False)
        kv_s = (k, v)
        outs_s.append((s.reshape(bs, N_HEADS, HEAD_DIM, D_STATE), c[:, CONV_HALO - (CONV_WIDTH - 1):]))
    stack = lambda outs, i: jnp.stack([o[i] for o in outs])
    heads = lambda a, b, s: a.reshape(depth, b, s, N_HEADS, HEAD_DIM)
    return (xp, xs,
            heads(kv_p[0], bp, sp), heads(kv_p[1], bp, sp), stack(outs_p, 0), stack(outs_p, 1),
            heads(kv_s[0], bs, ss), heads(kv_s[1], bs, ss), stack(outs_s, 0), stack(outs_s, 1))
```
